```python
import math
import jax, jax.numpy as jnp
from jax import lax
import numpy as np

D_MODEL = 4096
BATCH = 4
SEQ = 2048
DEPTH = 1
DEC_BATCH = 32
DEC_SEQ = 4
PAST_LEN = 8192
PAGE_SIZE = 128

F32 = jnp.float32
ATTN_HEADS = 16
ATTN_DH = 64
ATTN_DV = 2 * ATTN_DH
ATTN_WIDTH = ATTN_HEADS * ATTN_DV
QK_WIDTH = ATTN_HEADS * 2 * ATTN_DH
Q_BLOCK = 128
SSM_WIDTH = D_MODEL - ATTN_WIDTH
SSM_GROUP = 16
SSM_GROUPS = SSM_WIDTH // SSM_GROUP
SSM_STATE = 64
DT_MIN = 1e-3
DT_MAX = 1e-1
IN_WIDTH = 2 * QK_WIDTH + ATTN_WIDTH + SSM_WIDTH
MIX_WIDTH = ATTN_WIDTH + SSM_WIDTH
PEER_HEADS = 8
PEER_NKEYS = 128
PEER_EXPERTS = PEER_NKEYS * PEER_NKEYS
PEER_DKEY = 256
PEER_DHALF = PEER_DKEY // 2
PEER_TOPK = 16
PEER_BLOCK = 64
N_MOD = 6
EPS = 1e-6
NEG_INF = -1e30

kernel_name = "hymba_s5_diffattn_peer_step"


def rmsnorm(x, g):
    xf = x.astype(F32)
    y = xf * lax.rsqrt(jnp.mean(xf * xf, axis=-1, keepdims=True) + EPS)
    return (y * g.astype(F32)).astype(x.dtype)


def alibi_slopes(n):
    return jnp.asarray(np.array([2.0 ** (-8.0 * (h + 1) / n) for h in range(n)], np.float32))


def adaln(c, w_ada, b_ada):
    m = jax.nn.silu(c) @ w_ada + b_ada
    return jnp.split(m[:, None, :], N_MOD, axis=-1)


def modulate(x, g, shift, scale):
    return rmsnorm(x, g) * (1.0 + scale) + shift


def diff_attention(q, q_pos, keys, values, key_pos, lam, slopes):
    scale = ATTN_DH ** -0.5
    logits = []
    for k, kp in zip(keys, key_pos):
        s = jnp.einsum('bqhmd,bkhmd->bhmqk', q, k).astype(F32) * scale
        dist = (q_pos[:, None] - kp[None, :]).astype(F32)
        s = s - slopes[None, :, None, None, None] * dist
        logits.append(jnp.where(dist >= 0, s, NEG_INF))
    p = jax.nn.softmax(jnp.concatenate(logits, axis=-1), axis=-1)
    w = p[:, :, 0] - lam * p[:, :, 1]
    out = None
    start = 0
    for v in values:
        tk = v.shape[1]
        o = jnp.einsum('bhqk,bkhd->bqhd', w[..., start:start + tk].astype(v.dtype), v)
        out = o if out is None else out + o
        start += tk
    return out


def prompt_attention(q, k, v, lam, slopes):
    b, t = q.shape[0], q.shape[1]
    nb = t // Q_BLOCK
    qb = jnp.moveaxis(q.reshape(b, nb, Q_BLOCK, ATTN_HEADS, 2, ATTN_DH), 1, 0)
    kpos = jnp.arange(t, dtype=jnp.int32)

    def one_block(args):
        qi, i = args
        qpos = i * Q_BLOCK + jnp.arange(Q_BLOCK, dtype=jnp.int32)
        return diff_attention(qi, qpos, [k], [v], [kpos], lam, slopes)

    out = lax.map(one_block, (qb, jnp.arange(nb, dtype=jnp.int32)))
    return jnp.moveaxis(out, 0, 1).reshape(b, t, ATTN_HEADS, ATTN_DV)


def s5_discretize(lam_re, lam_im, log_dt, b_re, b_im):
    dt = jnp.exp(log_dt.astype(F32))[:, None]
    lr, li = lam_re.astype(F32), lam_im.astype(F32)
    mag = jnp.exp(lr * dt)
    a_re, a_im = mag * jnp.cos(li * dt), mag * jnp.sin(li * dt)
    den = lr * lr + li * li
    nr, ni = a_re - 1.0, a_im
    f_re, f_im = (nr * lr + ni * li) / den, (ni * lr - nr * li) / den
    br, bi = b_re.astype(F32), b_im.astype(F32)
    bb_re = f_re[..., None] * br - f_im[..., None] * bi
    bb_im = f_re[..., None] * bi + f_im[..., None] * br
    return a_re, a_im, bb_re, bb_im


def _complex_scan_combine(e1, e2):
    a1r, a1i, b1r, b1i = e1
    a2r, a2i, b2r, b2i = e2
    return (a1r * a2r - a1i * a2i, a1r * a2i + a1i * a2r,
            a2r * b1r - a2i * b1i + b2r, a2r * b1i + a2i * b1r + b2i)


def s5_mixer(u, h0_re, h0_im, lam_re, lam_im, log_dt, b_re, b_im, c_re, c_im, d_skip, w_glu):
    bsz, t = u.shape[0], u.shape[1]
    a_re, a_im, bb_re, bb_im = s5_discretize(lam_re, lam_im, log_dt, b_re, b_im)
    ug = u.reshape(bsz, t, SSM_GROUPS, SSM_GROUP).astype(F32)
    bu_re = jnp.einsum('btgc,gnc->btgn', ug, bb_re)
    bu_im = jnp.einsum('btgc,gnc->btgn', ug, bb_im)
    ar = jnp.broadcast_to(a_re, (1, t, SSM_GROUPS, SSM_STATE))
    ai = jnp.broadcast_to(a_im, (1, t, SSM_GROUPS, SSM_STATE))
    acum_re, acum_im, h_re, h_im = lax.associative_scan(
        _complex_scan_combine, (ar, ai, bu_re, bu_im), axis=1)
    h0r, h0i = h0_re.astype(F32)[:, None], h0_im.astype(F32)[:, None]
    h_re, h_im = (h_re + acum_re * h0r - acum_im * h0i,
                  h_im + acum_re * h0i + acum_im * h0r)
    y = (jnp.einsum('btgn,gcn->btgc', h_re, c_re.astype(F32))
         - jnp.einsum('btgn,gcn->btgc', h_im, c_im.astype(F32))
         + d_skip.astype(F32).reshape(SSM_GROUPS, SSM_GROUP) * ug)
    y = jax.nn.gelu(y.reshape(bsz, t, SSM_WIDTH)).astype(u.dtype)
    y = y * jax.nn.sigmoid(y @ w_glu)
    return y, h_re[:, -1], h_im[:, -1]


def peer_ffn(h, w_query, sub_keys, expert_u, expert_v):
    bsz, t, d = h.shape
    n = bsz * t
    nb = -(-n // PEER_BLOCK)
    xf = jnp.pad(h.reshape(n, d), ((0, nb * PEER_BLOCK - n), (0, 0))).reshape(nb, PEER_BLOCK, d)

    def one_block(xb):
        q = (xb @ w_query).reshape(PEER_BLOCK, PEER_HEADS, 2, PEER_DHALF)
        s = jnp.einsum('nhpd,hpkd->nhpk', q, sub_keys).astype(F32)
        sv, si = lax.top_k(s, PEER_TOPK)
        cand_s = (sv[:, :, 0, :, None] + sv[:, :, 1, None, :]).reshape(
            PEER_BLOCK, PEER_HEADS, PEER_TOPK * PEER_TOPK)
        cand_i = (si[:, :, 0, :, None] * PEER_NKEYS + si[:, :, 1, None, :]).reshape(
            PEER_BLOCK, PEER_HEADS, PEER_TOPK * PEER_TOPK)
        top_s, top_pos = lax.top_k(cand_s, PEER_TOPK)
        expert = jnp.take_along_axis(cand_i, top_pos, axis=-1)
        g = jax.nn.softmax(top_s, axis=-1)
        act = jax.nn.gelu(jnp.einsum('nhkd,nd->nhk', expert_u[expert], xb).astype(F32))
        coef = (g * act).astype(xb.dtype)
        return jnp.einsum('nhk,nhkd->nd', coef, expert_v[expert])

    out = lax.map(one_block, xf).reshape(nb * PEER_BLOCK, d)[:n]
    return out.reshape(bsz, t, d)


def setup_inputs(seed: int = 0) -> dict:
    key = jax.random.key(seed)
    ks = iter(jax.random.split(key, 48))

    def nrm(shape, std):
        return std * jax.random.normal(next(ks), shape, F32)

    n_pages = PAST_LEN // PAGE_SIZE
    n_used = DEC_BATCH * n_pages
    n_pool = n_used + (n_used + 3) // 4
    perm = jax.random.permutation(next(ks), n_pool)
    page_table = perm[:n_used].reshape(DEC_BATCH, n_pages).astype(jnp.int32)
    n_idx = jnp.arange(SSM_STATE, dtype=F32)
    return {
        'x_prompt': nrm((BATCH, SEQ, D_MODEL), 1.0),
        'x_sample': nrm((DEC_BATCH, DEC_SEQ, D_MODEL), 1.0),
        'cache_k': nrm((DEPTH, n_pool, PAGE_SIZE, ATTN_HEADS, 2 * ATTN_DH), 1.0),
        'cache_v': nrm((DEPTH, n_pool, PAGE_SIZE, ATTN_HEADS, ATTN_DV), 1.0),
        'page_table': page_table,
        'state_ssm_re': nrm((DEPTH, DEC_BATCH, SSM_GROUPS, SSM_STATE), 0.3),
        'state_ssm_im': nrm((DEPTH, DEC_BATCH, SSM_GROUPS, SSM_STATE), 0.3),
        'c_prompt': nrm((BATCH, D_MODEL), 1.0),
        'c_sample': nrm((DEC_BATCH, D_MODEL), 1.0),
        'w_ada': nrm((DEPTH, D_MODEL, N_MOD * D_MODEL), 0.5 * D_MODEL ** -0.5),
        'b_ada': nrm((DEPTH, N_MOD * D_MODEL), 0.02),
        'norm1_g': 1.0 + nrm((DEPTH, D_MODEL), 0.02),
        'norm2_g': 1.0 + nrm((DEPTH, D_MODEL), 0.02),
        'w_in': nrm((DEPTH, D_MODEL, IN_WIDTH), D_MODEL ** -0.5),
        'lam_q1': nrm((DEPTH, ATTN_DH), 0.1),
        'lam_k1': nrm((DEPTH, ATTN_DH), 0.1),
        'lam_q2': nrm((DEPTH, ATTN_DH), 0.1),
        'lam_k2': nrm((DEPTH, ATTN_DH), 0.1),
        'subln_g': 1.0 + nrm((DEPTH, ATTN_DV), 0.02),
        'ssm_lam_re': -0.5 + nrm((DEPTH, SSM_GROUPS, SSM_STATE), 0.01),
        'ssm_lam_im': jnp.pi * n_idx + nrm((DEPTH, SSM_GROUPS, SSM_STATE), 0.01),
        'ssm_log_dt': jax.random.uniform(next(ks), (DEPTH, SSM_GROUPS), F32,
                                         minval=math.log(DT_MIN), maxval=math.log(DT_MAX)),
        'ssm_b_re': nrm((DEPTH, SSM_GROUPS, SSM_STATE, SSM_GROUP), (2 * SSM_GROUP) ** -0.5),
        'ssm_b_im': nrm((DEPTH, SSM_GROUPS, SSM_STATE, SSM_GROUP), (2 * SSM_GROUP) ** -0.5),
        'ssm_c_re': nrm((DEPTH, SSM_GROUPS, SSM_GROUP, SSM_STATE), SSM_STATE ** -0.5),
        'ssm_c_im': nrm((DEPTH, SSM_GROUPS, SSM_GROUP, SSM_STATE), SSM_STATE ** -0.5),
        'ssm_d': nrm((DEPTH, SSM_WIDTH), 1.0),
        'w_glu': nrm((DEPTH, SSM_WIDTH, SSM_WIDTH), SSM_WIDTH ** -0.5),
        'w_out': nrm((DEPTH, MIX_WIDTH, D_MODEL), MIX_WIDTH ** -0.5),
        'peer_wq': nrm((DEPTH, D_MODEL, PEER_HEADS * PEER_DKEY), D_MODEL ** -0.5),
        'peer_sub_keys': nrm((DEPTH, PEER_HEADS, 2, PEER_NKEYS, PEER_DHALF), PEER_DHALF ** -0.5),
        'peer_u': nrm((DEPTH, PEER_EXPERTS, D_MODEL), D_MODEL ** -0.5),
        'peer_v': nrm((DEPTH, PEER_EXPERTS, D_MODEL), PEER_HEADS ** -0.5),
        'final_g': 1.0 + nrm((D_MODEL,), 0.02),
    }


def reference(x_prompt, x_sample, cache_k, cache_v, page_table, state_ssm_re, state_ssm_im,
              c_prompt, c_sample, w_ada, b_ada, norm1_g, norm2_g, w_in,
              lam_q1, lam_k1, lam_q2, lam_k2, subln_g,
              ssm_lam_re, ssm_lam_im, ssm_log_dt, ssm_b_re, ssm_b_im, ssm_c_re, ssm_c_im, ssm_d,
              w_glu, w_out, peer_wq, peer_sub_keys, peer_u, peer_v, final_g):
    slopes = alibi_slopes(ATTN_HEADS)
    xp, xs = x_prompt, x_sample
    kp_l, vp_l, ks_l, vs_l, hpr_l, hpi_l, hsr_l, hsi_l = ([] for _ in range(8))
    for l in range(DEPTH):
        lam_init = 0.8 - 0.6 * math.exp(-0.3 * l)
        lam = (jnp.exp(jnp.sum(lam_q1[l].astype(F32) * lam_k1[l].astype(F32)))
               - jnp.exp(jnp.sum(lam_q2[l].astype(F32) * lam_k2[l].astype(F32))) + lam_init)

        def attend_prompt(q, k, v):
            return prompt_attention(q, k, v, lam, slopes)

        def attend_sample(q, k, v):
            bd, td = q.shape[0], q.shape[1]
            k_past = cache_k[l, page_table].reshape(bd, -1, ATTN_HEADS, 2, ATTN_DH)
            v_past = cache_v[l, page_table].reshape(bd, -1, ATTN_HEADS, ATTN_DV)
            past = k_past.shape[1]
            q_pos = past + jnp.arange(td, dtype=jnp.int32)
            return diff_attention(q, q_pos, [k_past, k], [v_past, v],
                                  [jnp.arange(past, dtype=jnp.int32), q_pos], lam, slopes)

        def layer(x, c, h0_re, h0_im, attend):
            b, t = x.shape[0], x.shape[1]
            sh1, sc1, g1, sh2, sc2, g2 = adaln(c, w_ada[l], b_ada[l])
            h = modulate(x, norm1_g[l], sh1, sc1)
            proj = h @ w_in[l]
            q = proj[..., :QK_WIDTH].reshape(b, t, ATTN_HEADS, 2, ATTN_DH)
            k = proj[..., QK_WIDTH:2 * QK_WIDTH].reshape(b, t, ATTN_HEADS, 2, ATTN_DH)
            v = proj[..., 2 * QK_WIDTH:2 * QK_WIDTH + ATTN_WIDTH].reshape(b, t, ATTN_HEADS, ATTN_DV)
            u = proj[..., 2 * QK_WIDTH + ATTN_WIDTH:]
            o_attn = attend(q, k, v)
            o_attn = (rmsnorm(o_attn, subln_g[l]) * (1.0 - lam_init)).reshape(b, t, ATTN_WIDTH)
            o_ssm, hr, hi = s5_mixer(u, h0_re, h0_im, ssm_lam_re[l], ssm_lam_im[l], ssm_log_dt[l],
                                     ssm_b_re[l], ssm_b_im[l], ssm_c_re[l], ssm_c_im[l],
                                     ssm_d[l], w_glu[l])
            x = x + g1 * (jnp.concatenate([o_attn, o_ssm], axis=-1) @ w_out[l])
            h2 = modulate(x, norm2_g[l], sh2, sc2)
            x = x + g2 * peer_ffn(h2, peer_wq[l], peer_sub_keys[l], peer_u[l], peer_v[l])
            return x, k.reshape(b, t, ATTN_HEADS, 2 * ATTN_DH), v, hr, hi

        zeros_p = jnp.zeros((xp.shape[0], SSM_GROUPS, SSM_STATE), F32)
        xp, kp, vp, hpr, hpi = layer(xp, c_prompt, zeros_p, zeros_p, attend_prompt)
        xs, ks_, vs_, hsr, hsi = layer(xs, c_sample, state_ssm_re[l], state_ssm_im[l], attend_sample)
        kp_l.append(kp); vp_l.append(vp); ks_l.append(ks_); vs_l.append(vs_)
        hpr_l.append(hpr); hpi_l.append(hpi); hsr_l.append(hsr); hsi_l.append(hsi)

    y_prompt = rmsnorm(xp, final_g)
    y_sample = rmsnorm(xs, final_g)
    new_k_prompt = jnp.stack(kp_l)
    new_v_prompt = jnp.stack(vp_l)
    new_k_sample = jnp.stack(ks_l)
    new_v_sample = jnp.stack(vs_l)
    new_ssm_prompt_re = jnp.stack(hpr_l)
    new_ssm_prompt_im = jnp.stack(hpi_l)
    new_ssm_sample_re = jnp.stack(hsr_l)
    new_ssm_sample_im = jnp.stack(hsi_l)
    return (y_prompt, y_sample, new_k_prompt, new_v_prompt, new_k_sample, new_v_sample,
            new_ssm_prompt_re, new_ssm_prompt_im, new_ssm_sample_re, new_ssm_sample_im)
```

```python
import functools
import math

import numpy as np
import jax
import jax.numpy as jnp
from jax import lax
from jax.experimental import pallas as pl
from jax.experimental.pallas import tpu as pltpu

F32 = jnp.float32
BF16 = jnp.bfloat16

V7X_VMEM_BYTES = 64 * 1024 * 1024
LANES = 128
SUBLANES = 8

EPS = 1e-6
NEG_INF = -1e30

ATTN_HEADS = 16
ATTN_DH = 64
ATTN_DV = 2 * ATTN_DH
SSM_GROUP = 16
SSM_STATE = 64
GROUPS_PER_BLOCK = LANES // SSM_GROUP
STATE_BLOCK = GROUPS_PER_BLOCK * SSM_STATE
PROMPT_SEGMENTS = 8
PEER_HEADS = 8
PEER_NKEYS = 128
PEER_DHALF = 128
PEER_TOPK = 16
N_MOD = 6
NOT_SELECTED = 99.0


def _cparams(semantics, vmem_bytes):
    return pltpu.CompilerParams(dimension_semantics=semantics,
                                vmem_limit_bytes=min(int(vmem_bytes), V7X_VMEM_BYTES - (4 << 20)))


def _dot_nt(a, b):
    return lax.dot_general(a, b, (((1,), (1,)), ((), ())), preferred_element_type=F32)


def _rms(x):
    return x * lax.rsqrt(jnp.mean(x * x, axis=-1, keepdims=True) + EPS)


def _adaln_kernel(c_ref, w_ref, b_ref, o_ref):
    a = jax.nn.silu(c_ref[...])
    o_ref[...] = jnp.dot(a, w_ref[...], preferred_element_type=F32,
                         precision=lax.Precision.HIGHEST) + b_ref[...]


def _adaln(c, w, b, tn=512):
    m, k = c.shape
    n = w.shape[1]
    return pl.pallas_call(
        _adaln_kernel,
        grid=(n // tn,),
        in_specs=[pl.BlockSpec((m, k), lambda j: (0, 0)),
                  pl.BlockSpec((k, tn), lambda j: (0, j)),
                  pl.BlockSpec((1, tn), lambda j: (0, j))],
        out_specs=pl.BlockSpec((m, tn), lambda j: (0, j)),
        out_shape=jax.ShapeDtypeStruct((m, n), F32),
        compiler_params=_cparams(("arbitrary",), 2 * k * tn * 4 + (8 << 20)),
        name="adaln",
    )(c, w, b.reshape(1, n))


def _norm_mod_kernel(x_ref, g_ref, sh_ref, sc_ref, o_ref):
    y = _rms(x_ref[...]) * g_ref[...]
    o_ref[...] = (y * (1.0 + sc_ref[...]) + sh_ref[...]).astype(o_ref.dtype)


def _mod_spec(mod, tm, d):
    if mod.shape[1] == 1:
        return pl.BlockSpec((None, 1, d), lambda b, i: (b, 0, 0))
    return pl.BlockSpec((None, tm, d), lambda b, i: (b, i, 0))


def _norm_mod(x, g, shift, scale, tm):
    bsz, t, d = x.shape
    return pl.pallas_call(
        _norm_mod_kernel,
        grid=(bsz, t // tm),
        in_specs=[pl.BlockSpec((None, tm, d), lambda b, i: (b, i, 0)),
                  pl.BlockSpec((1, d), lambda b, i: (0, 0)),
                  _mod_spec(shift, tm, d), _mod_spec(scale, tm, d)],
        out_specs=pl.BlockSpec((None, tm, d), lambda b, i: (b, i, 0)),
        out_shape=jax.ShapeDtypeStruct((bsz, t, d), BF16),
        compiler_params=_cparams(("arbitrary", "arbitrary"), 8 * tm * d * 4 + (8 << 20)),
        name="norm_mod",
    )(x, g.reshape(1, d), shift, scale)


def _mm_kernel(a_ref, b_ref, *o_refs):
    acc = jnp.dot(a_ref[...], b_ref[...], preferred_element_type=F32)
    for o_ref in o_refs:
        o_ref[...] = acc.astype(o_ref.dtype)


def _matmul_cols(a, b, col0, ncols, out_dtypes, tm, tn):
    m, k = a.shape
    joff = col0 // tn
    out_bytes = sum(jnp.dtype(dt).itemsize for dt in out_dtypes)
    return pl.pallas_call(
        _mm_kernel,
        grid=(m // tm, ncols // tn),
        in_specs=[pl.BlockSpec((tm, k), lambda i, j: (i, 0)),
                  pl.BlockSpec((k, tn), lambda i, j: (0, j + joff))],
        out_specs=[pl.BlockSpec((tm, tn), lambda i, j: (i, j)) for _ in out_dtypes],
        out_shape=[jax.ShapeDtypeStruct((m, ncols), dt) for dt in out_dtypes],
        compiler_params=_cparams(("arbitrary", "arbitrary"),
                                 2 * (tm * k * 2 + k * tn * 2 + tm * tn * out_bytes)
                                 + tm * tn * 4 + (8 << 20)),
        name="matmul_cols",
    )(a, b)


def _prompt_attn_kernel(sc_ref, q_ref, k_ref, v_ref, g_ref, o_ref, m_ref, l_ref, acc_ref,
                        *, tq, out_scale):
    h = pl.program_id(1)
    qi = pl.program_id(2)
    lam = sc_ref[0]
    slope = sc_ref[1 + h]

    q = q_ref[...] * jnp.asarray(ATTN_DH ** -0.5, BF16)
    lane = lax.broadcasted_iota(jnp.int32, q.shape, 1)
    zero = jnp.zeros_like(q)
    qq = jnp.concatenate([jnp.where(lane < ATTN_DH, q, zero),
                          jnp.where(lane >= ATTN_DH, q, zero)], axis=0)

    m_ref[...] = jnp.full(m_ref.shape, NEG_INF, F32)
    l_ref[...] = jnp.zeros(l_ref.shape, F32)
    acc_ref[...] = jnp.zeros(acc_ref.shape, F32)
    col = lax.broadcasted_iota(jnp.int32, (1, tq), 1)

    def block(kj, masked):
        off = pl.multiple_of(kj * tq, tq)
        k = k_ref[pl.ds(off, tq), :]
        v = v_ref[pl.ds(off, tq), :]
        s = _dot_nt(qq, k)
        s = s + slope * (col + (kj - qi) * tq).astype(F32)
        if masked:
            row = lax.broadcasted_iota(jnp.int32, s.shape, 0)
            row = jnp.where(row >= tq, row - tq, row)
            s = jnp.where(row >= lax.broadcasted_iota(jnp.int32, s.shape, 1), s, NEG_INF)
        m_old = m_ref[...]
        m_new = jnp.maximum(m_old, jnp.max(s, axis=-1, keepdims=True))
        alpha = jnp.exp(m_old - m_new)
        p = jnp.exp(s - m_new)
        l_ref[...] = alpha * l_ref[...] + jnp.sum(p, axis=-1, keepdims=True)
        acc_ref[...] = alpha * acc_ref[...] + jnp.dot(p.astype(BF16), v,
                                                      preferred_element_type=F32)
        m_ref[...] = m_new

    def full_block(kj, carry):
        block(kj, False)
        return carry

    lax.fori_loop(0, qi, full_block, 0)
    block(qi, True)

    o = acc_ref[...] / l_ref[...]
    o = o[:tq] - lam * o[tq:]
    o_ref[...] = (_rms(o) * g_ref[...] * out_scale).astype(o_ref.dtype)


def _prompt_attention(scalars, q, k, v, subln_g, bsz, t, out_scale, tq):
    n, width = q.shape
    nq = t // tq
    return pl.pallas_call(
        functools.partial(_prompt_attn_kernel, tq=tq, out_scale=out_scale),
        grid=(bsz, ATTN_HEADS, nq),
        in_specs=[pl.BlockSpec(memory_space=pltpu.SMEM),
                  pl.BlockSpec((tq, ATTN_DV), lambda b, h, i: (b * nq + i, h)),
                  pl.BlockSpec((t, ATTN_DV), lambda b, h, i: (b, h)),
                  pl.BlockSpec((t, ATTN_DV), lambda b, h, i: (b, h)),
                  pl.BlockSpec((1, ATTN_DV), lambda b, h, i: (0, 0))],
        out_specs=pl.BlockSpec((tq, ATTN_DV), lambda b, h, i: (b * nq + i, h)),
        out_shape=jax.ShapeDtypeStruct((n, width), BF16),
        scratch_shapes=[pltpu.VMEM((2 * tq, 1), F32), pltpu.VMEM((2 * tq, 1), F32),
                        pltpu.VMEM((2 * tq, ATTN_DV), F32)],
        compiler_params=_cparams(("arbitrary", "arbitrary", "arbitrary"), 32 << 20),
        name="prompt_attn",
    )(scalars, q, k, v, subln_g.reshape(1, ATTN_DV))


ROWS_PER_HEAD = 2 * SUBLANES


def _decode_attn_kernel(pt_ref, sc_ref, q_ref, kc_ref, vc_ref, kn_ref, vn_ref, g_ref, o_ref,
                        m_ref, l_ref, acc_ref, *, n_pages, page, past_len, slopes, out_scale):
    del pt_ref
    p = pl.program_id(1)

    @pl.when(p == 0)
    def _():
        m_ref[...] = jnp.full(m_ref.shape, NEG_INF, F32)
        l_ref[...] = jnp.zeros(l_ref.shape, F32)
        acc_ref[...] = jnp.zeros(acc_ref.shape, F32)

    def process(k_ref, v_ref, base_pos, causal):
        col = lax.broadcasted_iota(jnp.int32, (ROWS_PER_HEAD, page), 1)
        rel = (col + base_pos).astype(F32)
        if causal:
            t = lax.broadcasted_iota(jnp.int32, (ROWS_PER_HEAD, page), 0) & (SUBLANES - 1)
            keep = col <= t
        for h in range(ATTN_HEADS):
            rows = slice(h * ROWS_PER_HEAD, (h + 1) * ROWS_PER_HEAD)
            cols = slice(h * ATTN_DV, (h + 1) * ATTN_DV)
            kh = k_ref[:, cols].astype(BF16)
            vh = v_ref[:, cols].astype(BF16)
            s = _dot_nt(q_ref[rows, :], kh) + slopes[h] * rel
            if causal:
                s = jnp.where(keep, s, NEG_INF)
            m_old = m_ref[rows, :]
            m_new = jnp.maximum(m_old, jnp.max(s, axis=-1, keepdims=True))
            alpha = jnp.exp(m_old - m_new)
            e = jnp.exp(s - m_new)
            l_ref[rows, :] = alpha * l_ref[rows, :] + jnp.sum(e, axis=-1, keepdims=True)
            acc_ref[rows, :] = alpha * acc_ref[rows, :] + jnp.dot(
                e.astype(BF16), vh, preferred_element_type=F32)
            m_ref[rows, :] = m_new

    @pl.when(p < n_pages)
    def _():
        process(kc_ref, vc_ref, p * page - past_len, False)

    @pl.when(p == n_pages)
    def _():
        process(kn_ref, vn_ref, 0, True)
        lam = sc_ref[0]
        for h in range(ATTN_HEADS):
            r0 = h * ROWS_PER_HEAD
            o = acc_ref[r0:r0 + ROWS_PER_HEAD, :] / l_ref[r0:r0 + ROWS_PER_HEAD, :]
            o = o[:SUBLANES] - lam * o[SUBLANES:]
            o_ref[h * SUBLANES:(h + 1) * SUBLANES, :] = _rms(o) * g_ref[...] * out_scale


def _decode_attention(page_table, scalars, q_rows, cache_k, cache_v, k_new, v_new, subln_g,
                      past_len, out_scale, slopes):
    bsz, n_pages = page_table.shape
    page, width = cache_k.shape[1], cache_k.shape[2]
    rows = ATTN_HEADS * ROWS_PER_HEAD

    def cache_map(b, p, pt):
        return (pt[b, jnp.minimum(p, n_pages - 1)], 0, 0)

    grid_spec = pltpu.PrefetchScalarGridSpec(
        num_scalar_prefetch=1,
        grid=(bsz, n_pages + 1),
        in_specs=[pl.BlockSpec(memory_space=pltpu.SMEM),
                  pl.BlockSpec((None, rows, ATTN_DV), lambda b, p, pt: (b, 0, 0)),
                  pl.BlockSpec((None, page, width), cache_map),
                  pl.BlockSpec((None, page, width), cache_map),
                  pl.BlockSpec((None, page, width), lambda b, p, pt: (b, 0, 0)),
                  pl.BlockSpec((None, page, width), lambda b, p, pt: (b, 0, 0)),
                  pl.BlockSpec((1, ATTN_DV), lambda b, p, pt: (0, 0))],
        out_specs=pl.BlockSpec((None, ATTN_HEADS * SUBLANES, ATTN_DV), lambda b, p, pt: (b, 0, 0)),
        scratch_shapes=[pltpu.VMEM((rows, 1), F32), pltpu.VMEM((rows, 1), F32),
                        pltpu.VMEM((rows, ATTN_DV), F32)],
    )
    return pl.pallas_call(
        functools.partial(_decode_attn_kernel, n_pages=n_pages, page=page, past_len=past_len,
                          slopes=slopes, out_scale=out_scale),
        grid_spec=grid_spec,
        out_shape=jax.ShapeDtypeStruct((bsz, ATTN_HEADS * SUBLANES, ATTN_DV), F32),
        compiler_params=_cparams(("arbitrary", "arbitrary"), 8 * page * width * 4 + (8 << 20)),
        name="decode_attn",
    )(page_table, scalars, q_rows, cache_k, cache_v, k_new, v_new, subln_g.reshape(1, ATTN_DV))


def _s5_kernel(*refs, tl, chain, emit_y):
    refs = list(refs)
    u_ref, bb_ref, a_ref, h0_ref = refs[:4]
    refs = refs[4:]
    if chain:
        aseg_ref = refs.pop(0)
    if emit_y:
        c_ref, d_ref = refs[:2]
        refs = refs[2:]
        y_ref = refs.pop(0)
    ht_ref = refs.pop(0)
    hs_ref = refs.pop(0)
    if chain:
        init_ref = refs.pop(0)

    n_steps, n_seq, _ = u_ref.shape
    sb = STATE_BLOCK
    ar = a_ref[0:1, :]
    ai = a_ref[1:2, :]

    if chain:
        seg_r = aseg_ref[0:1, :]
        seg_i = aseg_ref[1:2, :]
        for b in range(n_seq // PROMPT_SEGMENTS):
            r = jnp.zeros((1, sb), F32)
            i = jnp.zeros((1, sb), F32)
            for j in range(PROMPT_SEGMENTS):
                row = b * PROMPT_SEGMENTS + j
                init_ref[0, row:row + 1, :] = r
                init_ref[1, row:row + 1, :] = i
                er = h0_ref[0, row:row + 1, :]
                ei = h0_ref[1, row:row + 1, :]
                r, i = seg_r * r - seg_i * i + er, seg_r * i + seg_i * r + ei
        h_init = (init_ref[0], init_ref[1])
    else:
        h_init = (h0_ref[0], h0_ref[1])

    def chunk(c, carry):
        t0 = pl.multiple_of(c * tl, tl)
        x = u_ref[pl.ds(t0, tl)].reshape(tl * n_seq, LANES)
        hs_ref[...] = jnp.dot(x.astype(BF16), bb_ref[...], preferred_element_type=F32)

        def step(t, hc):
            hr, hi = hc
            r0 = pl.multiple_of(t * n_seq, n_seq)
            bur = hs_ref[pl.ds(r0, n_seq), 0:sb]
            bui = hs_ref[pl.ds(r0, n_seq), sb:2 * sb]
            nr = ar * hr - ai * hi + bur
            ni = ar * hi + ai * hr + bui
            if emit_y:
                hs_ref[pl.ds(r0, n_seq), 0:sb] = nr
                hs_ref[pl.ds(r0, n_seq), sb:2 * sb] = ni
            return nr, ni

        carry = lax.fori_loop(0, tl, step, carry)
        if emit_y:
            y = jnp.dot(hs_ref[...].astype(BF16), c_ref[...], preferred_element_type=F32)
            y = jax.nn.gelu(y + d_ref[...] * x)
            y_ref[pl.ds(t0, tl)] = y.reshape(tl, n_seq, LANES)
        return carry

    hr, hi = lax.fori_loop(0, n_steps // tl, chunk, h_init)
    ht_ref[0] = hr
    ht_ref[1] = hi


def _s5_scan(u_tm, bb_blk, a_blk, h0, aseg_blk, c_blk, d_blk, *, tl, chain, emit_y):
    n_steps, n_seq, width = u_tm.shape
    nblk = width // LANES
    sb = STATE_BLOCK
    in_specs = [pl.BlockSpec((n_steps, n_seq, LANES), lambda j: (0, 0, j)),
                pl.BlockSpec((None, LANES, 2 * sb), lambda j: (j, 0, 0)),
                pl.BlockSpec((None, 2, sb), lambda j: (j, 0, 0)),
                pl.BlockSpec((2, n_seq, sb), lambda j: (0, 0, j))]
    args = [u_tm, bb_blk, a_blk, h0]
    if chain:
        in_specs.append(pl.BlockSpec((None, 2, sb), lambda j: (j, 0, 0)))
        args.append(aseg_blk)
    out_specs = []
    out_shape = []
    if emit_y:
        in_specs += [pl.BlockSpec((None, 2 * sb, LANES), lambda j: (j, 0, 0)),
                     pl.BlockSpec((None, 1, LANES), lambda j: (j, 0, 0))]
        args += [c_blk, d_blk]
        out_specs.append(pl.BlockSpec((n_steps, n_seq, LANES), lambda j: (0, 0, j)))
        out_shape.append(jax.ShapeDtypeStruct((n_steps, n_seq, width), F32))
    out_specs.append(pl.BlockSpec((2, n_seq, sb), lambda j: (0, 0, j)))
    out_shape.append(jax.ShapeDtypeStruct((2, n_seq, nblk * sb), F32))
    scratch = [pltpu.VMEM((tl * n_seq, 2 * sb), F32)]
    if chain:
        scratch.append(pltpu.VMEM((2, n_seq, sb), F32))
    io_bytes = 2 * (1 + int(emit_y)) * n_steps * n_seq * LANES * 4
    return pl.pallas_call(
        functools.partial(_s5_kernel, tl=tl, chain=chain, emit_y=emit_y),
        grid=(nblk,),
        in_specs=in_specs, out_specs=out_specs, out_shape=out_shape,
        scratch_shapes=scratch,
        compiler_params=_cparams(("arbitrary",),
                                 io_bytes + 3 * tl * n_seq * 2 * sb * 4 + (12 << 20)),
        name="s5_scan",
    )(*args)


def _glu_kernel(y_ref, w_ref, o_ref):
    y = y_ref[...]
    z = jnp.dot(y.astype(BF16), w_ref[...], preferred_element_type=F32)
    o_ref[...] = (y * jax.nn.sigmoid(z)).astype(o_ref.dtype)


def _glu(y, w, tm):
    m, k = y.shape
    return pl.pallas_call(
        _glu_kernel,
        grid=(m // tm,),
        in_specs=[pl.BlockSpec((tm, k), lambda i: (i, 0)),
                  pl.BlockSpec((k, k), lambda i: (0, 0))],
        out_specs=pl.BlockSpec((tm, k), lambda i: (i, 0)),
        out_shape=jax.ShapeDtypeStruct((m, k), BF16),
        compiler_params=_cparams(("arbitrary",), 2 * k * k * 2 + 6 * tm * k * 4 + (8 << 20)),
        name="glu",
    )(y, w)


def _out_proj_kernel(a1_ref, a2_ref, w1_ref, w2_ref, x_ref, g_ref, o_ref):
    acc = jnp.dot(a1_ref[...], w1_ref[...], preferred_element_type=F32)
    acc = acc + jnp.dot(a2_ref[...], w2_ref[...], preferred_element_type=F32)
    o_ref[...] = x_ref[...] + g_ref[...] * acc


def _out_proj(a1, a2, w, x, gate, tm, tn):
    bsz, t, d = x.shape
    k1, k2 = a1.shape[-1], a2.shape[-1]
    a1 = a1.reshape(bsz, t, k1)
    a2 = a2.reshape(bsz, t, k2)
    if gate.shape[1] == 1:
        g_spec = pl.BlockSpec((None, 1, tn), lambda b, i, j: (b, 0, j))
    else:
        g_spec = pl.BlockSpec((None, tm, tn), lambda b, i, j: (b, i, j))
    return pl.pallas_call(
        _out_proj_kernel,
        grid=(bsz, t // tm, d // tn),
        in_specs=[pl.BlockSpec((None, tm, k1), lambda b, i, j: (b, i, 0)),
                  pl.BlockSpec((None, tm, k2), lambda b, i, j: (b, i, 0)),
                  pl.BlockSpec((k1, tn), lambda b, i, j: (0, j)),
                  pl.BlockSpec((k2, tn), lambda b, i, j: (k1 // k2, j)),
                  pl.BlockSpec((None, tm, tn), lambda b, i, j: (b, i, j)),
                  g_spec],
        out_specs=pl.BlockSpec((None, tm, tn), lambda b, i, j: (b, i, j)),
        out_shape=jax.ShapeDtypeStruct((bsz, t, d), F32),
        compiler_params=_cparams(("arbitrary", "arbitrary", "arbitrary"),
                                 2 * (tm * (k1 + k2) * 2 + (k1 + k2) * tn * 2 + 3 * tm * tn * 4)
                                 + (8 << 20)),
        name="out_proj",
    )(a1, a2, w, w, x, gate)


def _top_rows(s, k):
    n_rows = s.shape[0]
    row = lax.broadcasted_iota(jnp.int32, s.shape, 0).astype(F32)
    k_iota = lax.broadcasted_iota(jnp.int32, (k, s.shape[1]), 0)
    rank = jnp.full(s.shape, NOT_SELECTED, F32)
    val_mat = jnp.zeros((k, s.shape[1]), F32)
    vals, idxs = [], []
    for kk in range(k):
        mx = jnp.max(s, axis=0, keepdims=True)
        idx = jnp.min(jnp.where(s == mx, row, float(n_rows)), axis=0, keepdims=True)
        hit = row == idx
        rank = jnp.where(hit, float(kk), rank)
        s = jnp.where(hit, -jnp.inf, s)
        val_mat = jnp.where(k_iota == kk, mx, val_mat)
        vals.append(mx)
        idxs.append(idx)
    return vals, val_mat, idxs, rank


def _peer_route_kernel(h_ref, wq_ref, keys_ref, p0_ref, th_ref, p1_ref, r1_ref):
    q_t = _dot_nt(wq_ref[...], h_ref[...]).astype(BF16)
    s0 = jnp.dot(keys_ref[0], q_t[:PEER_DHALF], preferred_element_type=F32)
    s1 = jnp.dot(keys_ref[1], q_t[PEER_DHALF:], preferred_element_type=F32)
    v0, _, _, rank0 = _top_rows(s0, PEER_TOPK)
    v1, sv1, _, rank1 = _top_rows(s1, PEER_TOPK)
    tm = s0.shape[1]
    cand = jnp.concatenate([v0[a] + sv1 for a in range(PEER_TOPK)], axis=0)
    top_s, _, top_pos, _ = _top_rows(cand, PEER_TOPK)
    a_iota = lax.broadcasted_iota(jnp.int32, (PEER_TOPK, tm), 0).astype(F32)
    n_sel = jnp.zeros((PEER_TOPK, tm), F32)
    z = jnp.zeros((1, tm), F32)
    for kk in range(PEER_TOPK):
        n_sel = n_sel + jnp.where(a_iota == jnp.floor(top_pos[kk] * (1.0 / PEER_TOPK)), 1.0, 0.0)
        z = z + jnp.exp(top_s[kk] - top_s[0])
    th = jnp.zeros(s0.shape, F32)
    for a in range(PEER_TOPK):
        th = jnp.where(rank0 == float(a), n_sel[a:a + 1, :], th)
    p0_ref[...] = jnp.exp(s0 - v0[0])
    th_ref[...] = th
    p1_ref[...] = jnp.exp(s1 - v1[0]) / z
    r1_ref[...] = rank1


def _peer_route(h2, wq_t, keys, tm):
    n, d = h2.shape
    dk = 2 * PEER_DHALF
    spec_out = pl.BlockSpec((None, PEER_NKEYS, tm), lambda i, h: (h, 0, i))
    shape_out = jax.ShapeDtypeStruct((PEER_HEADS, PEER_NKEYS, n), F32)
    return pl.pallas_call(
        _peer_route_kernel,
        grid=(n // tm, PEER_HEADS),
        in_specs=[pl.BlockSpec((tm, d), lambda i, h: (i, 0)),
                  pl.BlockSpec((dk, d), lambda i, h: (h, 0)),
                  pl.BlockSpec((None, 2, PEER_NKEYS, PEER_DHALF), lambda i, h: (h, 0, 0, 0))],
        out_specs=[spec_out] * 4,
        out_shape=[shape_out] * 4,
        compiler_params=_cparams(("arbitrary", "arbitrary"),
                                 2 * (tm * d * 2 + dk * d * 2) + 40 * PEER_NKEYS * tm * 4 + (8 << 20)),
        name="peer_route",
    )(h2, wq_t, keys)


def _peer_dense_kernel(h_ref, u_ref, vt_ref, p0_ref, th_ref, p1_ref, r1_ref, o_ref, *, te):
    e = pl.program_id(1)

    @pl.when(e == 0)
    def _():
        o_ref[...] = jnp.zeros(o_ref.shape, F32)

    act = jax.nn.gelu(_dot_nt(u_ref[...], h_ref[...]))
    blocks = []
    for ib in range(te // PEER_NKEYS):
        i0 = e * (te // PEER_NKEYS) + ib
        w = jnp.zeros((PEER_NKEYS, act.shape[1]), F32)
        for h in range(PEER_HEADS):
            th = th_ref[h, pl.ds(i0, 1), :]
            p0 = p0_ref[h, pl.ds(i0, 1), :]
            w = w + jnp.where(r1_ref[h] < th, p1_ref[h] * p0, 0.0)
        blocks.append((w * act[ib * PEER_NKEYS:(ib + 1) * PEER_NKEYS]).astype(BF16))
    coef = jnp.concatenate(blocks, axis=0) if len(blocks) > 1 else blocks[0]
    o_ref[...] += jnp.dot(vt_ref[...], coef, preferred_element_type=F32)


def _peer_dense(h2, u, v_t, sel, tm, te):
    n, d = h2.shape
    n_exp = u.shape[0]
    sel_spec = pl.BlockSpec((PEER_HEADS, PEER_NKEYS, tm), lambda i, e: (0, 0, i))
    return pl.pallas_call(
        functools.partial(_peer_dense_kernel, te=te),
        grid=(n // tm, n_exp // te),
        in_specs=[pl.BlockSpec((tm, d), lambda i, e: (i, 0)),
                  pl.BlockSpec((te, d), lambda i, e: (e, 0)),
                  pl.BlockSpec((d, te), lambda i, e: (0, e))] + [sel_spec] * 4,
        out_specs=pl.BlockSpec((d, tm), lambda i, e: (0, i)),
        out_shape=jax.ShapeDtypeStruct((d, n), F32),
        compiler_params=_cparams(
            ("arbitrary", "arbitrary"),
            2 * (tm * d * 2 + 2 * te * d * 2 + 4 * PEER_HEADS * PEER_NKEYS * tm * 4 + d * tm * 4)
            + 6 * te * tm * 4 + (4 << 20)),
        name="peer_dense",
    )(h2, u, v_t, *sel)


def _peer_out_kernel(ot_ref, x_ref, g_ref, fg_ref, y_ref):
    x2 = x_ref[...] + g_ref[...] * ot_ref[...].T
    y_ref[...] = _rms(x2) * fg_ref[...]


def _peer_out(o_t, x, gate, final_g, tm):
    bsz, t, d = x.shape
    nt = t // tm
    return pl.pallas_call(
        _peer_out_kernel,
        grid=(bsz, nt),
        in_specs=[pl.BlockSpec((d, tm), lambda b, i: (0, b * nt + i)),
                  pl.BlockSpec((None, tm, d), lambda b, i: (b, i, 0)),
                  _mod_spec(gate, tm, d),
                  pl.BlockSpec((1, d), lambda b, i: (0, 0))],
        out_specs=pl.BlockSpec((None, tm, d), lambda b, i: (b, i, 0)),
        out_shape=jax.ShapeDtypeStruct((bsz, t, d), F32),
        compiler_params=_cparams(("arbitrary", "arbitrary"), 10 * tm * d * 4 + (8 << 20)),
        name="peer_out",
    )(o_t, x, gate, final_g.reshape(1, d))


def _alibi_slopes(n):
    return tuple(float(np.float32(2.0 ** (-8.0 * (h + 1) / n))) for h in range(n))


def _s5_params(lam_re, lam_im, log_dt, b_re, b_im, c_re, c_im, d_skip, seg_len):
    dt = jnp.exp(log_dt.astype(F32))[:, None]
    lr, li = lam_re.astype(F32), lam_im.astype(F32)
    mag = jnp.exp(lr * dt)
    a_re, a_im = mag * jnp.cos(li * dt), mag * jnp.sin(li * dt)
    den = lr * lr + li * li
    nr, ni = a_re - 1.0, a_im
    f_re, f_im = (nr * lr + ni * li) / den, (ni * lr - nr * li) / den
    br, bi = b_re.astype(F32), b_im.astype(F32)
    bb_re = f_re[..., None] * br - f_im[..., None] * bi
    bb_im = f_re[..., None] * bi + f_im[..., None] * br
    mag_s = jnp.exp(lr * dt * seg_len)
    s_re, s_im = mag_s * jnp.cos(li * dt * seg_len), mag_s * jnp.sin(li * dt * seg_len)

    g = lam_re.shape[0]
    nblk = g // GROUPS_PER_BLOCK
    eye = jnp.eye(GROUPS_PER_BLOCK, dtype=F32)

    def in_blocks(w):
        w = w.reshape(nblk, GROUPS_PER_BLOCK, SSM_STATE, SSM_GROUP)
        return jnp.einsum('jgnc,gk->jgckn', w, eye).reshape(nblk, LANES, STATE_BLOCK)

    def out_blocks(w):
        w = w.reshape(nblk, GROUPS_PER_BLOCK, SSM_GROUP, SSM_STATE)
        return jnp.einsum('jgcn,gk->jgnkc', w, eye).reshape(nblk, STATE_BLOCK, LANES)

    bb_blk = jnp.concatenate([in_blocks(bb_re), in_blocks(bb_im)], axis=2).astype(BF16)
    c_blk = jnp.concatenate([out_blocks(c_re.astype(F32)), -out_blocks(c_im.astype(F32))],
                            axis=1).astype(BF16)
    a_blk = jnp.stack([a_re.reshape(nblk, STATE_BLOCK), a_im.reshape(nblk, STATE_BLOCK)], axis=1)
    aseg_blk = jnp.stack([s_re.reshape(nblk, STATE_BLOCK), s_im.reshape(nblk, STATE_BLOCK)], axis=1)
    d_blk = d_skip.astype(F32).reshape(nblk, 1, LANES)
    return bb_blk, a_blk, aseg_blk, c_blk, d_blk


def _pick(n, pref):
    t = min(n, pref)
    while n % t:
        t //= 2
    return t


def kernel(x_prompt, x_sample, cache_k, cache_v, page_table, state_ssm_re, state_ssm_im,
           c_prompt, c_sample, w_ada, b_ada, norm1_g, norm2_g, w_in,
           lam_q1, lam_k1, lam_q2, lam_k2, subln_g,
           ssm_lam_re, ssm_lam_im, ssm_log_dt, ssm_b_re, ssm_b_im, ssm_c_re, ssm_c_im, ssm_d,
           w_glu, w_out, peer_wq, peer_sub_keys, peer_u, peer_v, final_g):
    depth = w_ada.shape[0]
    assert depth == 1, "kernel() fuses the final norm into the layer and supports DEPTH == 1"
    bsz, seq, d_model = x_prompt.shape
    dec_b, dec_t, _ = x_sample.shape
    n_p, n_s = bsz * seq, dec_b * dec_t
    n_pages, page = page_table.shape[1], cache_k.shape[2]
    past_len = n_pages * page
    attn_w = ATTN_HEADS * ATTN_DV
    ssm_w = w_in.shape[2] - 3 * attn_w
    n_groups = ssm_w // SSM_GROUP
    slopes = _alibi_slopes(ATTN_HEADS)
    seg_len = seq // PROMPT_SEGMENTS
    n_seq_p = bsz * PROMPT_SEGMENTS

    xp = x_prompt
    xs = x_sample.reshape(1, n_s, d_model)
    outs = [[] for _ in range(8)]

    for l in range(depth):
        lam_init = 0.8 - 0.6 * math.exp(-0.3 * l)
        out_scale = 1.0 - lam_init
        lam = (jnp.exp(jnp.sum(lam_q1[l].astype(F32) * lam_k1[l].astype(F32)))
               - jnp.exp(jnp.sum(lam_q2[l].astype(F32) * lam_k2[l].astype(F32))) + lam_init)
        scalars = jnp.concatenate([lam.reshape(1), jnp.asarray(slopes, F32)])

        c_all = jnp.concatenate([c_prompt, c_sample], axis=0)
        pad = (-c_all.shape[0]) % SUBLANES
        c_all = jnp.pad(c_all, ((0, pad), (0, 0)))
        mod = _adaln(c_all, w_ada[l], b_ada[l])
        mod_p = [m[:, None, :] for m in jnp.split(mod[:bsz], N_MOD, axis=-1)]
        mod_s = [jnp.repeat(m, dec_t, axis=0)[None] for m in
                 jnp.split(mod[bsz:bsz + dec_b], N_MOD, axis=-1)]

        w_in_b = w_in[l].astype(BF16)
        w_glu_b = w_glu[l].astype(BF16)
        w_out_b = w_out[l].astype(BF16)
        wq_t = peer_wq[l].T.astype(BF16)
        keys_b = peer_sub_keys[l].astype(BF16)
        u_b = peer_u[l].astype(BF16)
        v_t = peer_v[l].T.astype(BF16)
        bb_blk, a_blk, aseg_blk, c_blk, d_blk = _s5_params(
            ssm_lam_re[l], ssm_lam_im[l], ssm_log_dt[l], ssm_b_re[l], ssm_b_im[l],
            ssm_c_re[l], ssm_c_im[l], ssm_d[l], seg_len)

        def front(x, mods, tm):
            sh1, sc1 = mods[0], mods[1]
            h = _norm_mod(x, norm1_g[l], sh1, sc1, tm).reshape(-1, d_model)
            tmm = _pick(h.shape[0], 512)
            (q_b, q_f) = _matmul_cols(h, w_in_b, 0, attn_w, (BF16, F32), tmm, 512)
            (k_f, k_b) = _matmul_cols(h, w_in_b, attn_w, attn_w, (F32, BF16), tmm, 512)
            (v_f, v_b) = _matmul_cols(h, w_in_b, 2 * attn_w, attn_w, (F32, BF16), tmm, 512)
            (u_f,) = _matmul_cols(h, w_in_b, 3 * attn_w, ssm_w, (F32,), tmm, 512)
            return q_b, q_f, k_f, k_b, v_f, v_b, u_f

        def back(x, mods, o_attn, o_ssm, tm):
            g1, sh2, sc2, g2 = mods[2], mods[3], mods[4], mods[5]
            n = x.shape[0] * x.shape[1]
            x1 = _out_proj(o_attn, o_ssm, w_out_b, x, g1, _pick(x.shape[1], 512), 512)
            h2 = _norm_mod(x1, norm2_g[l], sh2, sc2, tm).reshape(n, d_model)
            tmp = _pick(n, 512)
            sel = _peer_route(h2, wq_t, keys_b, tmp)
            o_t = _peer_dense(h2, u_b, v_t, sel, tmp, 256)
            return _peer_out(o_t, x1, g2, final_g, _pick(x.shape[1], 256))

        q_b, _, k_f, k_b, v_f, v_b, u_f = front(xp, mod_p, 256)
        o_attn_p = _prompt_attention(scalars, q_b, k_b, v_b, subln_g[l], bsz, seq, out_scale, 256)
        u_tm = u_f.reshape(bsz, PROMPT_SEGMENTS, seg_len, ssm_w).transpose(2, 0, 1, 3)
        u_tm = u_tm.reshape(seg_len, n_seq_p, ssm_w)
        zeros_p = jnp.zeros((2, n_seq_p, n_groups * SSM_STATE), F32)
        (seg_end,) = _s5_scan(u_tm, bb_blk, a_blk, zeros_p, None, None, None,
                              tl=16, chain=False, emit_y=False)
        y_tm, h_end = _s5_scan(u_tm, bb_blk, a_blk, seg_end, aseg_blk, c_blk, d_blk,
                               tl=16, chain=True, emit_y=True)
        y_p = y_tm.reshape(seg_len, bsz, PROMPT_SEGMENTS, ssm_w).transpose(1, 2, 0, 3)
        o_ssm_p = _glu(y_p.reshape(n_p, ssm_w), w_glu_b, 256)
        y_prompt_l = back(xp, mod_p, o_attn_p, o_ssm_p, 256)
        h_end = h_end.reshape(2, bsz, PROMPT_SEGMENTS, n_groups, SSM_STATE)[:, :, -1]
        outs[0].append(k_f.reshape(bsz, seq, ATTN_HEADS, ATTN_DV))
        outs[1].append(v_f.reshape(bsz, seq, ATTN_HEADS, ATTN_DV))
        outs[4].append(h_end[0])
        outs[5].append(h_end[1])

        _, q_f, k_f, _, v_f, _, u_f = front(xs, mod_s, n_s)
        qh = q_f.reshape(dec_b, dec_t, ATTN_HEADS, ATTN_DV).transpose(0, 2, 1, 3)
        lane = jnp.arange(ATTN_DV) < ATTN_DH
        qh = jnp.stack([jnp.where(lane, qh, 0.0), jnp.where(lane, 0.0, qh)], axis=2)
        qh = jnp.pad(qh, ((0, 0), (0, 0), (0, 0), (0, SUBLANES - dec_t), (0, 0)))
        q_rows = (qh * (ATTN_DH ** -0.5)).astype(BF16).reshape(
            dec_b, ATTN_HEADS * ROWS_PER_HEAD, ATTN_DV)
        k_new = jnp.pad(k_f.reshape(dec_b, dec_t, attn_w), ((0, 0), (0, page - dec_t), (0, 0)))
        v_new = jnp.pad(v_f.reshape(dec_b, dec_t, attn_w), ((0, 0), (0, page - dec_t), (0, 0)))
        o_rows = _decode_attention(
            page_table, scalars, q_rows,
            cache_k[l].reshape(-1, page, attn_w), cache_v[l].reshape(-1, page, attn_w),
            k_new, v_new, subln_g[l], past_len, out_scale, slopes)
        o_attn_s = o_rows.reshape(dec_b, ATTN_HEADS, SUBLANES, ATTN_DV)[:, :, :dec_t]
        o_attn_s = o_attn_s.transpose(0, 2, 1, 3).reshape(n_s, attn_w).astype(BF16)
        u_tm = u_f.reshape(dec_b, dec_t, ssm_w).transpose(1, 0, 2)
        h0 = jnp.stack([state_ssm_re[l].reshape(dec_b, -1), state_ssm_im[l].reshape(dec_b, -1)])
        y_tm, h_end = _s5_scan(u_tm, bb_blk, a_blk, h0.astype(F32), None, c_blk, d_blk,
                               tl=dec_t, chain=False, emit_y=True)
        o_ssm_s = _glu(y_tm.transpose(1, 0, 2).reshape(n_s, ssm_w), w_glu_b, n_s)
        y_sample_l = back(xs, mod_s, o_attn_s, o_ssm_s, n_s)
        outs[2].append(k_f.reshape(dec_b, dec_t, ATTN_HEADS, ATTN_DV))
        outs[3].append(v_f.reshape(dec_b, dec_t, ATTN_HEADS, ATTN_DV))
        outs[6].append(h_end[0].reshape(dec_b, n_groups, SSM_STATE))
        outs[7].append(h_end[1].reshape(dec_b, n_groups, SSM_STATE))

    y_prompt = y_prompt_l
    y_sample = y_sample_l.reshape(dec_b, dec_t, d_model)
    return (y_prompt, y_sample) + tuple(jnp.stack(o) for o in outs)
```

```python
import functools
import math

import numpy as np
import jax
import jax.numpy as jnp
from jax import lax
from jax.experimental import pallas as pl
from jax.experimental.pallas import tpu as pltpu

F32 = jnp.float32
BF16 = jnp.bfloat16

V7X_VMEM_BYTES = 64 * 1024 * 1024
LANES = 128
SUBLANES = 8

EPS = 1e-6
NEG_INF = -1e30

ATTN_HEADS = 16
ATTN_DH = 64
ATTN_DV = 2 * ATTN_DH
SSM_GROUP = 16
SSM_STATE = 64
GROUPS_PER_BLOCK = LANES // SSM_GROUP
STATE_BLOCK = GROUPS_PER_BLOCK * SSM_STATE
PROMPT_SEGMENTS = 8
PEER_HEADS = 8
PEER_NKEYS = 128
PEER_DHALF = 128
PEER_TOPK = 16
N_MOD = 6
NOT_SELECTED = 99.0


def _cparams(semantics, vmem_bytes):
    return pltpu.CompilerParams(dimension_semantics=semantics,
                                vmem_limit_bytes=min(int(vmem_bytes), V7X_VMEM_BYTES - (4 << 20)))


def _dot_nt(a, b):
    return lax.dot_general(a, b, (((1,), (1,)), ((), ())), preferred_element_type=F32)


def _rms(x):
    return x * lax.rsqrt(jnp.mean(x * x, axis=-1, keepdims=True) + EPS)


def _adaln_kernel(c_ref, w_ref, b_ref, o_ref):
    a = jax.nn.silu(c_ref[...])
    o_ref[...] = jnp.dot(a, w_ref[...], preferred_element_type=F32,
                         precision=lax.Precision.HIGHEST) + b_ref[...]


def _adaln(c, w, b, tn=512):
    m, k = c.shape
    n = w.shape[1]
    return pl.pallas_call(
        _adaln_kernel,
        grid=(n // tn,),
        in_specs=[pl.BlockSpec((m, k), lambda j: (0, 0)),
                  pl.BlockSpec((k, tn), lambda j: (0, j)),
                  pl.BlockSpec((1, tn), lambda j: (0, j))],
        out_specs=pl.BlockSpec((m, tn), lambda j: (0, j)),
        out_shape=jax.ShapeDtypeStruct((m, n), F32),
        compiler_params=_cparams(("arbitrary",), 2 * k * tn * 4 + (8 << 20)),
        name="adaln",
    )(c, w, b.reshape(1, n))


def _norm_mod_kernel(x_ref, g_ref, sh_ref, sc_ref, o_ref):
    y = _rms(x_ref[...]) * g_ref[...]
    o_ref[...] = (y * (1.0 + sc_ref[...]) + sh_ref[...]).astype(o_ref.dtype)


def _mod_spec(mod, tm, d):
    if mod.shape[1] == 1:
        return pl.BlockSpec((None, 1, d), lambda b, i: (b, 0, 0))
    return pl.BlockSpec((None, tm, d), lambda b, i: (b, i, 0))


def _norm_mod(x, g, shift, scale, tm):
    bsz, t, d = x.shape
    return pl.pallas_call(
        _norm_mod_kernel,
        grid=(bsz, t // tm),
        in_specs=[pl.BlockSpec((None, tm, d), lambda b, i: (b, i, 0)),
                  pl.BlockSpec((1, d), lambda b, i: (0, 0)),
                  _mod_spec(shift, tm, d), _mod_spec(scale, tm, d)],
        out_specs=pl.BlockSpec((None, tm, d), lambda b, i: (b, i, 0)),
        out_shape=jax.ShapeDtypeStruct((bsz, t, d), BF16),
        compiler_params=_cparams(("arbitrary", "arbitrary"), 8 * tm * d * 4 + (8 << 20)),
        name="norm_mod",
    )(x, g.reshape(1, d), shift, scale)


def _mm_kernel(a_ref, b_ref, *o_refs):
    acc = jnp.dot(a_ref[...], b_ref[...], preferred_element_type=F32)
    for o_ref in o_refs:
        o_ref[...] = acc.astype(o_ref.dtype)


def _matmul_cols(a, b, col0, ncols, out_dtypes, tm, tn):
    m, k = a.shape
    joff = col0 // tn
    out_bytes = sum(jnp.dtype(dt).itemsize for dt in out_dtypes)
    return pl.pallas_call(
        _mm_kernel,
        grid=(m // tm, ncols // tn),
        in_specs=[pl.BlockSpec((tm, k), lambda i, j: (i, 0)),
                  pl.BlockSpec((k, tn), lambda i, j: (0, j + joff))],
        out_specs=[pl.BlockSpec((tm, tn), lambda i, j: (i, j)) for _ in out_dtypes],
        out_shape=[jax.ShapeDtypeStruct((m, ncols), dt) for dt in out_dtypes],
        compiler_params=_cparams(("arbitrary", "arbitrary"),
                                 2 * (tm * k * 2 + k * tn * 2 + tm * tn * out_bytes)
                                 + tm * tn * 4 + (8 << 20)),
        name="matmul_cols",
    )(a, b)


def _prompt_attn_kernel(sc_ref, q_ref, k_ref, v_ref, g_ref, o_ref, m_ref, l_ref, acc_ref,
                        *, tq, hp, out_scale):
    hg = pl.program_id(1)
    qi = pl.program_id(2)
    lam = sc_ref[0]
    slopes = [sc_ref[1 + hg * hp + j] for j in range(hp)]

    def head_cols(j):
        return slice(j * ATTN_DV, (j + 1) * ATTN_DV)

    qqs = []
    for j in range(hp):
        q = q_ref[:, head_cols(j)] * jnp.asarray(ATTN_DH ** -0.5, BF16)
        lane = lax.broadcasted_iota(jnp.int32, q.shape, 1)
        zero = jnp.zeros_like(q)
        qqs.append(jnp.concatenate([jnp.where(lane < ATTN_DH, q, zero),
                                    jnp.where(lane >= ATTN_DH, q, zero)], axis=0))

    m_ref[...] = jnp.full(m_ref.shape, NEG_INF, F32)
    l_ref[...] = jnp.zeros(l_ref.shape, F32)
    acc_ref[...] = jnp.zeros(acc_ref.shape, F32)
    col = lax.broadcasted_iota(jnp.int32, (1, tq), 1)
    lane_reps = tq // LANES

    def block(kj, masked):
        off = pl.multiple_of(kj * tq, tq)
        rel = (col + (kj - qi) * tq).astype(F32)
        for j in range(hp):
            k = k_ref[pl.ds(off, tq), head_cols(j)]
            v = v_ref[pl.ds(off, tq), head_cols(j)]
            s = _dot_nt(qqs[j], k) + slopes[j] * rel
            if masked:
                row = lax.broadcasted_iota(jnp.int32, s.shape, 0)
                row = jnp.where(row >= tq, row - tq, row)
                s = jnp.where(row >= lax.broadcasted_iota(jnp.int32, s.shape, 1), s, NEG_INF)
            m_old = m_ref[j]
            m_new = jnp.maximum(m_old, jnp.max(s, axis=-1, keepdims=True))
            alpha = jnp.exp(m_old - m_new)
            p = jnp.exp(s - jnp.concatenate([m_new] * lane_reps, axis=1))
            l_ref[j] = alpha * l_ref[j] + jnp.sum(p, axis=-1, keepdims=True)
            acc_ref[j] = alpha * acc_ref[j] + jnp.dot(p.astype(BF16), v,
                                                      preferred_element_type=F32)
            m_ref[j] = m_new

    def full_block(kj, carry):
        block(kj, False)
        return carry

    lax.fori_loop(0, qi, full_block, 0)
    block(qi, True)

    for j in range(hp):
        o = acc_ref[j] / l_ref[j]
        o = o[:tq] - lam * o[tq:]
        o_ref[:, head_cols(j)] = (_rms(o) * g_ref[...] * out_scale).astype(o_ref.dtype)


def _prompt_attention(scalars, q, k, v, subln_g, bsz, t, out_scale, tq, hp):
    n, width = q.shape
    nq = t // tq
    assert ATTN_DV == LANES and tq % LANES == 0 and ATTN_HEADS % hp == 0
    wb = hp * ATTN_DV
    return pl.pallas_call(
        functools.partial(_prompt_attn_kernel, tq=tq, hp=hp, out_scale=out_scale),
        grid=(bsz, ATTN_HEADS // hp, nq),
        in_specs=[pl.BlockSpec(memory_space=pltpu.SMEM),
                  pl.BlockSpec((tq, wb), lambda b, h, i: (b * nq + i, h)),
                  pl.BlockSpec((t, wb), lambda b, h, i: (b, h)),
                  pl.BlockSpec((t, wb), lambda b, h, i: (b, h)),
                  pl.BlockSpec((1, ATTN_DV), lambda b, h, i: (0, 0))],
        out_specs=pl.BlockSpec((tq, wb), lambda b, h, i: (b * nq + i, h)),
        out_shape=jax.ShapeDtypeStruct((n, width), BF16),
        scratch_shapes=[pltpu.VMEM((hp, 2 * tq, LANES), F32), pltpu.VMEM((hp, 2 * tq, LANES), F32),
                        pltpu.VMEM((hp, 2 * tq, ATTN_DV), F32)],
        compiler_params=_cparams(("arbitrary", "arbitrary", "arbitrary"),
                                 8 * t * wb * 2 + 16 * hp * tq * tq * 4 + (8 << 20)),
        name="prompt_attn",
    )(scalars, q, k, v, subln_g.reshape(1, ATTN_DV))


ROWS_PER_HEAD = 2 * SUBLANES


def _decode_attn_kernel(pt_ref, sc_ref, q_ref, slope_ref, kc_ref, vc_ref, kn_ref, vn_ref, g_ref,
                        o_ref, m_ref, l_ref, acc_ref, *, n_pages, page, past_len, out_scale):
    del pt_ref
    p = pl.program_id(1)

    @pl.when(p == 0)
    def _():
        m_ref[...] = jnp.full(m_ref.shape, NEG_INF, F32)
        l_ref[...] = jnp.zeros(l_ref.shape, F32)
        acc_ref[...] = jnp.zeros(acc_ref.shape, F32)

    def head_rows(h):
        return slice(h * ROWS_PER_HEAD, (h + 1) * ROWS_PER_HEAD)

    def head_keys(ref, h):
        return ref[pl.ds(h, page, stride=ATTN_HEADS), :].astype(BF16)

    def process(k_ref, v_ref, base_pos, causal):
        col = lax.broadcasted_iota(jnp.int32, (1, page), 1)
        rel = (col + base_pos).astype(F32)
        s = jnp.concatenate([_dot_nt(q_ref[head_rows(h), :], head_keys(k_ref, h))
                             for h in range(ATTN_HEADS)], axis=0)
        s = s + slope_ref[...] * rel
        if causal:
            t = lax.broadcasted_iota(jnp.int32, s.shape, 0) & (SUBLANES - 1)
            s = jnp.where(lax.broadcasted_iota(jnp.int32, s.shape, 1) <= t, s, NEG_INF)
        m_old = m_ref[...]
        m_new = jnp.maximum(m_old, jnp.max(s, axis=-1, keepdims=True))
        alpha = jnp.exp(m_old - m_new)
        e = jnp.exp(s - m_new)
        l_ref[...] = alpha * l_ref[...] + jnp.sum(e, axis=-1, keepdims=True)
        eb = e.astype(BF16)
        pv = jnp.concatenate([jnp.dot(eb[head_rows(h), :], head_keys(v_ref, h),
                                      preferred_element_type=F32)
                              for h in range(ATTN_HEADS)], axis=0)
        acc_ref[...] = alpha * acc_ref[...] + pv
        m_ref[...] = m_new

    @pl.when(p < n_pages)
    def _():
        process(kc_ref, vc_ref, p * page - past_len, False)

    @pl.when(p == n_pages)
    def _():
        process(kn_ref, vn_ref, 0, True)
        lam = sc_ref[0]
        o = acc_ref[...] / l_ref[...]
        for h in range(ATTN_HEADS):
            r0 = h * ROWS_PER_HEAD
            oh = o[r0:r0 + SUBLANES] - lam * o[r0 + SUBLANES:r0 + ROWS_PER_HEAD]
            o_ref[h * SUBLANES:(h + 1) * SUBLANES, :] = _rms(oh) * g_ref[...] * out_scale


def _decode_attention(page_table, scalars, q_rows, cache_k, cache_v, k_new, v_new, subln_g,
                      page, past_len, out_scale, slopes):
    bsz, n_pages = page_table.shape
    page_rows = cache_k.shape[1]
    rows = ATTN_HEADS * ROWS_PER_HEAD
    assert page == ATTN_DV == LANES and page_rows == page * ATTN_HEADS
    slope_rows = jnp.broadcast_to(
        jnp.repeat(jnp.asarray(slopes, F32), ROWS_PER_HEAD)[:, None], (rows, page))

    def cache_map(b, p, pt):
        return (pt[b, jnp.minimum(p, n_pages - 1)], 0, 0)

    grid_spec = pltpu.PrefetchScalarGridSpec(
        num_scalar_prefetch=1,
        grid=(bsz, n_pages + 1),
        in_specs=[pl.BlockSpec(memory_space=pltpu.SMEM),
                  pl.BlockSpec((None, rows, ATTN_DV), lambda b, p, pt: (b, 0, 0)),
                  pl.BlockSpec((rows, page), lambda b, p, pt: (0, 0)),
                  pl.BlockSpec((None, page_rows, ATTN_DV), cache_map),
                  pl.BlockSpec((None, page_rows, ATTN_DV), cache_map),
                  pl.BlockSpec((None, page_rows, ATTN_DV), lambda b, p, pt: (b, 0, 0)),
                  pl.BlockSpec((None, page_rows, ATTN_DV), lambda b, p, pt: (b, 0, 0)),
                  pl.BlockSpec((1, ATTN_DV), lambda b, p, pt: (0, 0))],
        out_specs=pl.BlockSpec((None, ATTN_HEADS * SUBLANES, ATTN_DV), lambda b, p, pt: (b, 0, 0)),
        scratch_shapes=[pltpu.VMEM((rows, page), F32), pltpu.VMEM((rows, page), F32),
                        pltpu.VMEM((rows, ATTN_DV), F32)],
    )
    return pl.pallas_call(
        functools.partial(_decode_attn_kernel, n_pages=n_pages, page=page, past_len=past_len,
                          out_scale=out_scale),
        grid_spec=grid_spec,
        out_shape=jax.ShapeDtypeStruct((bsz, ATTN_HEADS * SUBLANES, ATTN_DV), F32),
        compiler_params=_cparams(("arbitrary", "arbitrary"),
                                 8 * page_rows * ATTN_DV * 4 + (8 << 20)),
        name="decode_attn",
    )(page_table, scalars, q_rows, slope_rows, cache_k, cache_v, k_new, v_new,
      subln_g.reshape(1, ATTN_DV))


def _s5_kernel(*refs, tl, chain, emit_y):
    refs = list(refs)
    u_ref, bb_ref, a_ref, h0_ref = refs[:4]
    refs = refs[4:]
    if chain:
        aseg_ref = refs.pop(0)
    if emit_y:
        c_ref, d_ref = refs[:2]
        refs = refs[2:]
        y_ref = refs.pop(0)
    ht_ref = refs.pop(0)
    hs_ref = refs.pop(0)
    if chain:
        init_ref = refs.pop(0)

    n_steps, n_seq, _ = u_ref.shape
    sb = STATE_BLOCK
    ar = a_ref[0:1, :]
    ai = a_ref[1:2, :]

    if chain:
        seg_r = aseg_ref[0:1, :]
        seg_i = aseg_ref[1:2, :]
        for b in range(n_seq // PROMPT_SEGMENTS):
            r = jnp.zeros((1, sb), F32)
            i = jnp.zeros((1, sb), F32)
            for j in range(PROMPT_SEGMENTS):
                row = b * PROMPT_SEGMENTS + j
                init_ref[0, row:row + 1, :] = r
                init_ref[1, row:row + 1, :] = i
                er = h0_ref[0, row:row + 1, :]
                ei = h0_ref[1, row:row + 1, :]
                r, i = seg_r * r - seg_i * i + er, seg_r * i + seg_i * r + ei
        h_init = (init_ref[0], init_ref[1])
    else:
        h_init = (h0_ref[0], h0_ref[1])

    def chunk(c, carry):
        t0 = pl.multiple_of(c * tl, tl)
        x = u_ref[pl.ds(t0, tl)].reshape(tl * n_seq, LANES)
        hs_ref[...] = jnp.dot(x.astype(BF16), bb_ref[...], preferred_element_type=F32)

        def step(t, hc):
            hr, hi = hc
            r0 = pl.multiple_of(t * n_seq, n_seq)
            bur = hs_ref[pl.ds(r0, n_seq), 0:sb]
            bui = hs_ref[pl.ds(r0, n_seq), sb:2 * sb]
            nr = ar * hr - ai * hi + bur
            ni = ar * hi + ai * hr + bui
            if emit_y:
                hs_ref[pl.ds(r0, n_seq), 0:sb] = nr
                hs_ref[pl.ds(r0, n_seq), sb:2 * sb] = ni
            return nr, ni

        carry = lax.fori_loop(0, tl, step, carry)
        if emit_y:
            y = jnp.dot(hs_ref[...].astype(BF16), c_ref[...], preferred_element_type=F32)
            y = jax.nn.gelu(y + d_ref[...] * x)
            y_ref[pl.ds(t0, tl)] = y.reshape(tl, n_seq, LANES)
        return carry

    hr, hi = lax.fori_loop(0, n_steps // tl, chunk, h_init)
    ht_ref[0] = hr
    ht_ref[1] = hi


def _s5_scan(u_tm, bb_blk, a_blk, h0, aseg_blk, c_blk, d_blk, *, tl, chain, emit_y):
    n_steps, n_seq, width = u_tm.shape
    nblk = width // LANES
    sb = STATE_BLOCK
    in_specs = [pl.BlockSpec((n_steps, n_seq, LANES), lambda j: (0, 0, j)),
                pl.BlockSpec((None, LANES, 2 * sb), lambda j: (j, 0, 0)),
                pl.BlockSpec((None, 2, sb), lambda j: (j, 0, 0)),
                pl.BlockSpec((2, n_seq, sb), lambda j: (0, 0, j))]
    args = [u_tm, bb_blk, a_blk, h0]
    if chain:
        in_specs.append(pl.BlockSpec((None, 2, sb), lambda j: (j, 0, 0)))
        args.append(aseg_blk)
    out_specs = []
    out_shape = []
    if emit_y:
        in_specs += [pl.BlockSpec((None, 2 * sb, LANES), lambda j: (j, 0, 0)),
                     pl.BlockSpec((None, 1, LANES), lambda j: (j, 0, 0))]
        args += [c_blk, d_blk]
        out_specs.append(pl.BlockSpec((n_steps, n_seq, LANES), lambda j: (0, 0, j)))
        out_shape.append(jax.ShapeDtypeStruct((n_steps, n_seq, width), F32))
    out_specs.append(pl.BlockSpec((2, n_seq, sb), lambda j: (0, 0, j)))
    out_shape.append(jax.ShapeDtypeStruct((2, n_seq, nblk * sb), F32))
    scratch = [pltpu.VMEM((tl * n_seq, 2 * sb), F32)]
    if chain:
        scratch.append(pltpu.VMEM((2, n_seq, sb), F32))
    io_bytes = 2 * (1 + int(emit_y)) * n_steps * n_seq * LANES * 4
    return pl.pallas_call(
        functools.partial(_s5_kernel, tl=tl, chain=chain, emit_y=emit_y),
        grid=(nblk,),
        in_specs=in_specs, out_specs=out_specs, out_shape=out_shape,
        scratch_shapes=scratch,
        compiler_params=_cparams(("arbitrary",),
                                 io_bytes + 3 * tl * n_seq * 2 * sb * 4 + (12 << 20)),
        name="s5_scan",
    )(*args)


def _glu_kernel(y_ref, w_ref, o_ref):
    y = y_ref[...]
    z = jnp.dot(y.astype(BF16), w_ref[...], preferred_element_type=F32)
    o_ref[...] = (y * jax.nn.sigmoid(z)).astype(o_ref.dtype)


def _glu(y, w, tm):
    m, k = y.shape
    return pl.pallas_call(
        _glu_kernel,
        grid=(m // tm,),
        in_specs=[pl.BlockSpec((tm, k), lambda i: (i, 0)),
                  pl.BlockSpec((k, k), lambda i: (0, 0))],
        out_specs=pl.BlockSpec((tm, k), lambda i: (i, 0)),
        out_shape=jax.ShapeDtypeStruct((m, k), BF16),
        compiler_params=_cparams(("arbitrary",), 2 * k * k * 2 + 6 * tm * k * 4 + (8 << 20)),
        name="glu",
    )(y, w)


def _out_proj_kernel(a1_ref, a2_ref, w1_ref, w2_ref, x_ref, g_ref, o_ref):
    acc = jnp.dot(a1_ref[...], w1_ref[...], preferred_element_type=F32)
    acc = acc + jnp.dot(a2_ref[...], w2_ref[...], preferred_element_type=F32)
    o_ref[...] = x_ref[...] + g_ref[...] * acc


def _out_proj(a1, a2, w, x, gate, tm, tn):
    bsz, t, d = x.shape
    k1, k2 = a1.shape[-1], a2.shape[-1]
    a1 = a1.reshape(bsz, t, k1)
    a2 = a2.reshape(bsz, t, k2)
    if gate.shape[1] == 1:
        g_spec = pl.BlockSpec((None, 1, tn), lambda b, i, j: (b, 0, j))
    else:
        g_spec = pl.BlockSpec((None, tm, tn), lambda b, i, j: (b, i, j))
    return pl.pallas_call(
        _out_proj_kernel,
        grid=(bsz, t // tm, d // tn),
        in_specs=[pl.BlockSpec((None, tm, k1), lambda b, i, j: (b, i, 0)),
                  pl.BlockSpec((None, tm, k2), lambda b, i, j: (b, i, 0)),
                  pl.BlockSpec((k1, tn), lambda b, i, j: (0, j)),
                  pl.BlockSpec((k2, tn), lambda b, i, j: (k1 // k2, j)),
                  pl.BlockSpec((None, tm, tn), lambda b, i, j: (b, i, j)),
                  g_spec],
        out_specs=pl.BlockSpec((None, tm, tn), lambda b, i, j: (b, i, j)),
        out_shape=jax.ShapeDtypeStruct((bsz, t, d), F32),
        compiler_params=_cparams(("arbitrary", "arbitrary", "arbitrary"),
                                 2 * (tm * (k1 + k2) * 2 + (k1 + k2) * tn * 2 + 3 * tm * tn * 4)
                                 + (8 << 20)),
        name="out_proj",
    )(a1, a2, w, w, x, gate)


def _top_rows(s, k):
    n_rows = s.shape[0]
    row = lax.broadcasted_iota(jnp.int32, s.shape, 0).astype(F32)
    k_iota = lax.broadcasted_iota(jnp.int32, (k, s.shape[1]), 0)
    rank = jnp.full(s.shape, NOT_SELECTED, F32)
    val_mat = jnp.zeros((k, s.shape[1]), F32)
    vals, idxs = [], []
    for kk in range(k):
        mx = jnp.max(s, axis=0, keepdims=True)
        idx = jnp.min(jnp.where(s == mx, row, float(n_rows)), axis=0, keepdims=True)
        hit = row == idx
        rank = jnp.where(hit, float(kk), rank)
        s = jnp.where(hit, -jnp.inf, s)
        val_mat = jnp.where(k_iota == kk, mx, val_mat)
        vals.append(mx)
        idxs.append(idx)
    return vals, val_mat, idxs, rank


def _peer_route_kernel(h_ref, wq_ref, keys_ref, p0_ref, th_ref, p1_ref, r1_ref):
    q_t = _dot_nt(wq_ref[...], h_ref[...]).astype(BF16)
    s0 = jnp.dot(keys_ref[0], q_t[:PEER_DHALF], preferred_element_type=F32)
    s1 = jnp.dot(keys_ref[1], q_t[PEER_DHALF:], preferred_element_type=F32)
    v0, _, _, rank0 = _top_rows(s0, PEER_TOPK)
    v1, sv1, _, rank1 = _top_rows(s1, PEER_TOPK)
    tm = s0.shape[1]
    cand = jnp.concatenate([v0[a] + sv1 for a in range(PEER_TOPK)], axis=0)
    top_s, _, top_pos, _ = _top_rows(cand, PEER_TOPK)
    a_iota = lax.broadcasted_iota(jnp.int32, (PEER_TOPK, tm), 0).astype(F32)
    n_sel = jnp.zeros((PEER_TOPK, tm), F32)
    z = jnp.zeros((1, tm), F32)
    for kk in range(PEER_TOPK):
        n_sel = n_sel + jnp.where(a_iota == jnp.floor(top_pos[kk] * (1.0 / PEER_TOPK)), 1.0, 0.0)
        z = z + jnp.exp(top_s[kk] - top_s[0])
    th = jnp.zeros(s0.shape, F32)
    for a in range(PEER_TOPK):
        th = jnp.where(rank0 == float(a), n_sel[a:a + 1, :], th)
    p0_ref[...] = jnp.exp(s0 - v0[0])
    th_ref[...] = th
    p1_ref[...] = jnp.exp(s1 - v1[0]) / z
    r1_ref[...] = rank1


def _peer_route(h2, wq_t, keys, tm):
    n, d = h2.shape
    dk = 2 * PEER_DHALF
    spec_out = pl.BlockSpec((None, PEER_NKEYS, tm), lambda i, h: (h, 0, i))
    shape_out = jax.ShapeDtypeStruct((PEER_HEADS, PEER_NKEYS, n), F32)
    return pl.pallas_call(
        _peer_route_kernel,
        grid=(n // tm, PEER_HEADS),
        in_specs=[pl.BlockSpec((tm, d), lambda i, h: (i, 0)),
                  pl.BlockSpec((dk, d), lambda i, h: (h, 0)),
                  pl.BlockSpec((None, 2, PEER_NKEYS, PEER_DHALF), lambda i, h: (h, 0, 0, 0))],
        out_specs=[spec_out] * 4,
        out_shape=[shape_out] * 4,
        compiler_params=_cparams(("arbitrary", "arbitrary"),
                                 2 * (tm * d * 2 + dk * d * 2) + 40 * PEER_NKEYS * tm * 4 + (8 << 20)),
        name="peer_route",
    )(h2, wq_t, keys)


def _peer_dense_kernel(h_ref, u_ref, vt_ref, p0_ref, th_ref, p1_ref, r1_ref, o_ref, *, te):
    e = pl.program_id(1)

    @pl.when(e == 0)
    def _():
        o_ref[...] = jnp.zeros(o_ref.shape, F32)

    act = jax.nn.gelu(_dot_nt(u_ref[...], h_ref[...]))
    blocks = []
    for ib in range(te // PEER_NKEYS):
        i0 = e * (te // PEER_NKEYS) + ib
        w = jnp.zeros((PEER_NKEYS, act.shape[1]), F32)
        for h in range(PEER_HEADS):
            th = th_ref[h, pl.ds(i0, 1), :]
            p0 = p0_ref[h, pl.ds(i0, 1), :]
            w = w + jnp.where(r1_ref[h] < th, p1_ref[h] * p0, 0.0)
        blocks.append((w * act[ib * PEER_NKEYS:(ib + 1) * PEER_NKEYS]).astype(BF16))
    coef = jnp.concatenate(blocks, axis=0) if len(blocks) > 1 else blocks[0]
    o_ref[...] += jnp.dot(vt_ref[...], coef, preferred_element_type=F32)


def _peer_dense(h2, u, v_t, sel, tm, te):
    n, d = h2.shape
    n_exp = u.shape[0]
    once = pl.Buffered(1)
    sel_spec = pl.BlockSpec((PEER_HEADS, PEER_NKEYS, tm), lambda i, e: (0, 0, i),
                            pipeline_mode=once)
    return pl.pallas_call(
        functools.partial(_peer_dense_kernel, te=te),
        grid=(n // tm, n_exp // te),
        in_specs=[pl.BlockSpec((tm, d), lambda i, e: (i, 0), pipeline_mode=once),
                  pl.BlockSpec((te, d), lambda i, e: (e, 0)),
                  pl.BlockSpec((d, te), lambda i, e: (0, e))] + [sel_spec] * 4,
        out_specs=pl.BlockSpec((d, tm), lambda i, e: (0, i), pipeline_mode=once),
        out_shape=jax.ShapeDtypeStruct((d, n), F32),
        compiler_params=_cparams(
            ("arbitrary", "arbitrary"),
            tm * d * 2 + 2 * 2 * te * d * 2 + 4 * PEER_HEADS * PEER_NKEYS * tm * 4 + d * tm * 4
            + 8 * te * tm * 4 + (6 << 20)),
        name="peer_dense",
    )(h2, u, v_t, *sel)


def _peer_out_kernel(ot_ref, x_ref, g_ref, fg_ref, y_ref):
    x2 = x_ref[...] + g_ref[...] * ot_ref[...].T
    y_ref[...] = _rms(x2) * fg_ref[...]


def _peer_out(o_t, x, gate, final_g, tm):
    bsz, t, d = x.shape
    nt = t // tm
    return pl.pallas_call(
        _peer_out_kernel,
        grid=(bsz, nt),
        in_specs=[pl.BlockSpec((d, tm), lambda b, i: (0, b * nt + i)),
                  pl.BlockSpec((None, tm, d), lambda b, i: (b, i, 0)),
                  _mod_spec(gate, tm, d),
                  pl.BlockSpec((1, d), lambda b, i: (0, 0))],
        out_specs=pl.BlockSpec((None, tm, d), lambda b, i: (b, i, 0)),
        out_shape=jax.ShapeDtypeStruct((bsz, t, d), F32),
        compiler_params=_cparams(("arbitrary", "arbitrary"), 10 * tm * d * 4 + (8 << 20)),
        name="peer_out",
    )(o_t, x, gate, final_g.reshape(1, d))


def _alibi_slopes(n):
    return tuple(float(np.float32(2.0 ** (-8.0 * (h + 1) / n))) for h in range(n))


def _s5_params(lam_re, lam_im, log_dt, b_re, b_im, c_re, c_im, d_skip, seg_len):
    dt = jnp.exp(log_dt.astype(F32))[:, None]
    lr, li = lam_re.astype(F32), lam_im.astype(F32)
    mag = jnp.exp(lr * dt)
    a_re, a_im = mag * jnp.cos(li * dt), mag * jnp.sin(li * dt)
    den = lr * lr + li * li
    nr, ni = a_re - 1.0, a_im
    f_re, f_im = (nr * lr + ni * li) / den, (ni * lr - nr * li) / den
    br, bi = b_re.astype(F32), b_im.astype(F32)
    bb_re = f_re[..., None] * br - f_im[..., None] * bi
    bb_im = f_re[..., None] * bi + f_im[..., None] * br
    mag_s = jnp.exp(lr * dt * seg_len)
    s_re, s_im = mag_s * jnp.cos(li * dt * seg_len), mag_s * jnp.sin(li * dt * seg_len)

    g = lam_re.shape[0]
    nblk = g // GROUPS_PER_BLOCK
    eye = jnp.eye(GROUPS_PER_BLOCK, dtype=F32)

    def in_blocks(w):
        w = w.reshape(nblk, GROUPS_PER_BLOCK, SSM_STATE, SSM_GROUP)
        return jnp.einsum('jgnc,gk->jgckn', w, eye).reshape(nblk, LANES, STATE_BLOCK)

    def out_blocks(w):
        w = w.reshape(nblk, GROUPS_PER_BLOCK, SSM_GROUP, SSM_STATE)
        return jnp.einsum('jgcn,gk->jgnkc', w, eye).reshape(nblk, STATE_BLOCK, LANES)

    bb_blk = jnp.concatenate([in_blocks(bb_re), in_blocks(bb_im)], axis=2).astype(BF16)
    c_blk = jnp.concatenate([out_blocks(c_re.astype(F32)), -out_blocks(c_im.astype(F32))],
                            axis=1).astype(BF16)
    a_blk = jnp.stack([a_re.reshape(nblk, STATE_BLOCK), a_im.reshape(nblk, STATE_BLOCK)], axis=1)
    aseg_blk = jnp.stack([s_re.reshape(nblk, STATE_BLOCK), s_im.reshape(nblk, STATE_BLOCK)], axis=1)
    d_blk = d_skip.astype(F32).reshape(nblk, 1, LANES)
    return bb_blk, a_blk, aseg_blk, c_blk, d_blk


def _pick(n, pref):
    t = min(n, pref)
    while n % t:
        t //= 2
    return t


def kernel(x_prompt, x_sample, cache_k, cache_v, page_table, state_ssm_re, state_ssm_im,
           c_prompt, c_sample, w_ada, b_ada, norm1_g, norm2_g, w_in,
           lam_q1, lam_k1, lam_q2, lam_k2, subln_g,
           ssm_lam_re, ssm_lam_im, ssm_log_dt, ssm_b_re, ssm_b_im, ssm_c_re, ssm_c_im, ssm_d,
           w_glu, w_out, peer_wq, peer_sub_keys, peer_u, peer_v, final_g):
    depth = w_ada.shape[0]
    assert depth == 1, "kernel() fuses the final norm into the layer and supports DEPTH == 1"
    bsz, seq, d_model = x_prompt.shape
    dec_b, dec_t, _ = x_sample.shape
    n_p, n_s = bsz * seq, dec_b * dec_t
    n_pages, page = page_table.shape[1], cache_k.shape[2]
    past_len = n_pages * page
    attn_w = ATTN_HEADS * ATTN_DV
    ssm_w = w_in.shape[2] - 3 * attn_w
    n_groups = ssm_w // SSM_GROUP
    slopes = _alibi_slopes(ATTN_HEADS)
    seg_len = seq // PROMPT_SEGMENTS
    n_seq_p = bsz * PROMPT_SEGMENTS

    xp = x_prompt
    xs = x_sample.reshape(1, n_s, d_model)
    outs = [[] for _ in range(8)]

    for l in range(depth):
        lam_init = 0.8 - 0.6 * math.exp(-0.3 * l)
        out_scale = 1.0 - lam_init
        lam = (jnp.exp(jnp.sum(lam_q1[l].astype(F32) * lam_k1[l].astype(F32)))
               - jnp.exp(jnp.sum(lam_q2[l].astype(F32) * lam_k2[l].astype(F32))) + lam_init)
        scalars = jnp.concatenate([lam.reshape(1), jnp.asarray(slopes, F32)])

        c_all = jnp.concatenate([c_prompt, c_sample], axis=0)
        pad = (-c_all.shape[0]) % SUBLANES
        c_all = jnp.pad(c_all, ((0, pad), (0, 0)))
        mod = _adaln(c_all, w_ada[l], b_ada[l])
        mod_p = [m[:, None, :] for m in jnp.split(mod[:bsz], N_MOD, axis=-1)]
        mod_s = [jnp.repeat(m, dec_t, axis=0)[None] for m in
                 jnp.split(mod[bsz:bsz + dec_b], N_MOD, axis=-1)]

        w_in_b = w_in[l].astype(BF16)
        w_glu_b = w_glu[l].astype(BF16)
        w_out_b = w_out[l].astype(BF16)
        wq_t = peer_wq[l].T.astype(BF16)
        keys_b = peer_sub_keys[l].astype(BF16)
        u_b = peer_u[l].astype(BF16)
        v_t = peer_v[l].T.astype(BF16)
        bb_blk, a_blk, aseg_blk, c_blk, d_blk = _s5_params(
            ssm_lam_re[l], ssm_lam_im[l], ssm_log_dt[l], ssm_b_re[l], ssm_b_im[l],
            ssm_c_re[l], ssm_c_im[l], ssm_d[l], seg_len)

        def front(x, mods, tm):
            sh1, sc1 = mods[0], mods[1]
            h = _norm_mod(x, norm1_g[l], sh1, sc1, tm).reshape(-1, d_model)
            tmm = _pick(h.shape[0], 512)
            (q_b, q_f) = _matmul_cols(h, w_in_b, 0, attn_w, (BF16, F32), tmm, 512)
            (k_f, k_b) = _matmul_cols(h, w_in_b, attn_w, attn_w, (F32, BF16), tmm, 512)
            (v_f, v_b) = _matmul_cols(h, w_in_b, 2 * attn_w, attn_w, (F32, BF16), tmm, 512)
            (u_f,) = _matmul_cols(h, w_in_b, 3 * attn_w, ssm_w, (F32,), tmm, 512)
            return q_b, q_f, k_f, k_b, v_f, v_b, u_f

        def back(x, mods, o_attn, o_ssm, tm):
            g1, sh2, sc2, g2 = mods[2], mods[3], mods[4], mods[5]
            n = x.shape[0] * x.shape[1]
            x1 = _out_proj(o_attn, o_ssm, w_out_b, x, g1, _pick(x.shape[1], 512), 512)
            h2 = _norm_mod(x1, norm2_g[l], sh2, sc2, tm).reshape(n, d_model)
            tmp = _pick(n, 512)
            sel = _peer_route(h2, wq_t, keys_b, tmp)
            o_t = _peer_dense(h2, u_b, v_t, sel, tmp, 512)
            return _peer_out(o_t, x1, g2, final_g, _pick(x.shape[1], 256))

        q_b, _, k_f, k_b, v_f, v_b, u_f = front(xp, mod_p, 256)
        o_attn_p = _prompt_attention(scalars, q_b, k_b, v_b, subln_g[l], bsz, seq, out_scale,
                                     256, 2)
        u_tm = u_f.reshape(bsz, PROMPT_SEGMENTS, seg_len, ssm_w).transpose(2, 0, 1, 3)
        u_tm = u_tm.reshape(seg_len, n_seq_p, ssm_w)
        zeros_p = jnp.zeros((2, n_seq_p, n_groups * SSM_STATE), F32)
        (seg_end,) = _s5_scan(u_tm, bb_blk, a_blk, zeros_p, None, None, None,
                              tl=16, chain=False, emit_y=False)
        y_tm, h_end = _s5_scan(u_tm, bb_blk, a_blk, seg_end, aseg_blk, c_blk, d_blk,
                               tl=16, chain=True, emit_y=True)
        y_p = y_tm.reshape(seg_len, bsz, PROMPT_SEGMENTS, ssm_w).transpose(1, 2, 0, 3)
        o_ssm_p = _glu(y_p.reshape(n_p, ssm_w), w_glu_b, 256)
        y_prompt_l = back(xp, mod_p, o_attn_p, o_ssm_p, 256)
        h_end = h_end.reshape(2, bsz, PROMPT_SEGMENTS, n_groups, SSM_STATE)[:, :, -1]
        outs[0].append(k_f.reshape(bsz, seq, ATTN_HEADS, ATTN_DV))
        outs[1].append(v_f.reshape(bsz, seq, ATTN_HEADS, ATTN_DV))
        outs[4].append(h_end[0])
        outs[5].append(h_end[1])

        _, q_f, k_f, _, v_f, _, u_f = front(xs, mod_s, n_s)
        qh = q_f.reshape(dec_b, dec_t, ATTN_HEADS, ATTN_DV).transpose(0, 2, 1, 3)
        lane = jnp.arange(ATTN_DV) < ATTN_DH
        qh = jnp.stack([jnp.where(lane, qh, 0.0), jnp.where(lane, 0.0, qh)], axis=2)
        qh = jnp.pad(qh, ((0, 0), (0, 0), (0, 0), (0, SUBLANES - dec_t), (0, 0)))
        q_rows = (qh * (ATTN_DH ** -0.5)).astype(BF16).reshape(
            dec_b, ATTN_HEADS * ROWS_PER_HEAD, ATTN_DV)
        new_pad = ((0, 0), (0, (page - dec_t) * ATTN_HEADS), (0, 0))
        k_new = jnp.pad(k_f.reshape(dec_b, dec_t * ATTN_HEADS, ATTN_DV), new_pad)
        v_new = jnp.pad(v_f.reshape(dec_b, dec_t * ATTN_HEADS, ATTN_DV), new_pad)
        o_rows = _decode_attention(
            page_table, scalars, q_rows,
            cache_k.reshape(-1, page * ATTN_HEADS, ATTN_DV),
            cache_v.reshape(-1, page * ATTN_HEADS, ATTN_DV),
            k_new, v_new, subln_g[l], page, past_len, out_scale, slopes)
        o_attn_s = o_rows.reshape(dec_b, ATTN_HEADS, SUBLANES, ATTN_DV)[:, :, :dec_t]
        o_attn_s = o_attn_s.transpose(0, 2, 1, 3).reshape(n_s, attn_w).astype(BF16)
        u_tm = u_f.reshape(dec_b, dec_t, ssm_w).transpose(1, 0, 2)
        h0 = jnp.stack([state_ssm_re[l].reshape(dec_b, -1), state_ssm_im[l].reshape(dec_b, -1)])
        y_tm, h_end = _s5_scan(u_tm, bb_blk, a_blk, h0.astype(F32), None, c_blk, d_blk,
                               tl=dec_t, chain=False, emit_y=True)
        o_ssm_s = _glu(y_tm.transpose(1, 0, 2).reshape(n_s, ssm_w), w_glu_b, n_s)
        y_sample_l = back(xs, mod_s, o_attn_s, o_ssm_s, n_s)
        outs[2].append(k_f.reshape(dec_b, dec_t, ATTN_HEADS, ATTN_DV))
        outs[3].append(v_f.reshape(dec_b, dec_t, ATTN_HEADS, ATTN_DV))
        outs[6].append(h_end[0].reshape(dec_b, n_groups, SSM_STATE))
        outs[7].append(h_end[1].reshape(dec_b, n_groups, SSM_STATE))

    y_prompt = y_prompt_l
    y_sample = y_sample_l.reshape(dec_b, dec_t, d_model)
    return (y_prompt, y_sample) + tuple(jnp.stack(o) for o in outs)
```

```python
import functools
import math

import numpy as np
import jax
import jax.numpy as jnp
from jax import lax
from jax.experimental import pallas as pl
from jax.experimental.pallas import tpu as pltpu

F32 = jnp.float32
BF16 = jnp.bfloat16

V7X_VMEM_BYTES = 64 * 1024 * 1024
LANES = 128
SUBLANES = 8

EPS = 1e-6
NEG_INF = -1e30

ATTN_HEADS = 16
ATTN_DH = 64
ATTN_DV = 2 * ATTN_DH
SSM_GROUP = 16
SSM_STATE = 64
GROUPS_PER_BLOCK = LANES // SSM_GROUP
STATE_BLOCK = GROUPS_PER_BLOCK * SSM_STATE
PROMPT_SEGMENTS = 8
PEER_HEADS = 8
PEER_NKEYS = 128
PEER_DHALF = 128
PEER_TOPK = 16
N_MOD = 6
NOT_SELECTED = 99.0


def _cparams(semantics, vmem_bytes):
    return pltpu.CompilerParams(dimension_semantics=semantics,
                                vmem_limit_bytes=min(int(vmem_bytes), V7X_VMEM_BYTES - (4 << 20)))


def _dot_nt(a, b):
    return lax.dot_general(a, b, (((1,), (1,)), ((), ())), preferred_element_type=F32)


def _rms(x):
    return x * lax.rsqrt(jnp.mean(x * x, axis=-1, keepdims=True) + EPS)


def _adaln_kernel(c_ref, w_ref, b_ref, o_ref):
    a = jax.nn.silu(c_ref[...])
    o_ref[...] = jnp.dot(a, w_ref[...], preferred_element_type=F32,
                         precision=lax.Precision.HIGHEST) + b_ref[...]


def _adaln(c, w, b, tn=512):
    m, k = c.shape
    n = w.shape[1]
    return pl.pallas_call(
        _adaln_kernel,
        grid=(n // tn,),
        in_specs=[pl.BlockSpec((m, k), lambda j: (0, 0)),
                  pl.BlockSpec((k, tn), lambda j: (0, j)),
                  pl.BlockSpec((1, tn), lambda j: (0, j))],
        out_specs=pl.BlockSpec((m, tn), lambda j: (0, j)),
        out_shape=jax.ShapeDtypeStruct((m, n), F32),
        compiler_params=_cparams(("arbitrary",), 2 * k * tn * 4 + (8 << 20)),
        name="adaln",
    )(c, w, b.reshape(1, n))


def _norm_mod_kernel(x_ref, g_ref, sh_ref, sc_ref, o_ref):
    y = _rms(x_ref[...]) * g_ref[...]
    o_ref[...] = (y * (1.0 + sc_ref[...]) + sh_ref[...]).astype(o_ref.dtype)


def _mod_spec(mod, tm, d):
    if mod.shape[1] == 1:
        return pl.BlockSpec((None, 1, d), lambda b, i: (b, 0, 0))
    return pl.BlockSpec((None, tm, d), lambda b, i: (b, i, 0))


def _norm_mod(x, g, shift, scale, tm):
    bsz, t, d = x.shape
    return pl.pallas_call(
        _norm_mod_kernel,
        grid=(bsz, t // tm),
        in_specs=[pl.BlockSpec((None, tm, d), lambda b, i: (b, i, 0)),
                  pl.BlockSpec((1, d), lambda b, i: (0, 0)),
                  _mod_spec(shift, tm, d), _mod_spec(scale, tm, d)],
        out_specs=pl.BlockSpec((None, tm, d), lambda b, i: (b, i, 0)),
        out_shape=jax.ShapeDtypeStruct((bsz, t, d), BF16),
        compiler_params=_cparams(("arbitrary", "arbitrary"), 8 * tm * d * 4 + (8 << 20)),
        name="norm_mod",
    )(x, g.reshape(1, d), shift, scale)


def _mm_kernel(a_ref, b_ref, *o_refs):
    acc = jnp.dot(a_ref[...], b_ref[...], preferred_element_type=F32)
    for o_ref in o_refs:
        o_ref[...] = acc.astype(o_ref.dtype)


def _matmul_cols(a, b, col0, ncols, out_dtypes, tm, tn):
    m, k = a.shape
    joff = col0 // tn
    out_bytes = sum(jnp.dtype(dt).itemsize for dt in out_dtypes)
    return pl.pallas_call(
        _mm_kernel,
        grid=(m // tm, ncols // tn),
        in_specs=[pl.BlockSpec((tm, k), lambda i, j: (i, 0)),
                  pl.BlockSpec((k, tn), lambda i, j: (0, j + joff))],
        out_specs=[pl.BlockSpec((tm, tn), lambda i, j: (i, j)) for _ in out_dtypes],
        out_shape=[jax.ShapeDtypeStruct((m, ncols), dt) for dt in out_dtypes],
        compiler_params=_cparams(("arbitrary", "arbitrary"),
                                 2 * (tm * k * 2 + k * tn * 2 + tm * tn * out_bytes)
                                 + tm * tn * 4 + (8 << 20)),
        name="matmul_cols",
    )(a, b)


def _prompt_attn_kernel(sc_ref, q_ref, k_ref, v_ref, g_ref, o_ref, m_ref, l_ref, acc_ref,
                        *, tq, hp, out_scale):
    hg = pl.program_id(1)
    qi = pl.program_id(2)
    lam = sc_ref[0]
    slopes = [sc_ref[1 + hg * hp + j] for j in range(hp)]

    def head_cols(j):
        return slice(j * ATTN_DV, (j + 1) * ATTN_DV)

    qqs = []
    for j in range(hp):
        q = q_ref[:, head_cols(j)] * jnp.asarray(ATTN_DH ** -0.5, BF16)
        lane = lax.broadcasted_iota(jnp.int32, q.shape, 1)
        zero = jnp.zeros_like(q)
        qqs.append(jnp.concatenate([jnp.where(lane < ATTN_DH, q, zero),
                                    jnp.where(lane >= ATTN_DH, q, zero)], axis=0))

    m_ref[...] = jnp.full(m_ref.shape, NEG_INF, F32)
    l_ref[...] = jnp.zeros(l_ref.shape, F32)
    acc_ref[...] = jnp.zeros(acc_ref.shape, F32)
    col = lax.broadcasted_iota(jnp.int32, (1, tq), 1)
    lane_reps = tq // LANES

    def block(kj, masked):
        off = pl.multiple_of(kj * tq, tq)
        rel = (col + (kj - qi) * tq).astype(F32)
        for j in range(hp):
            k = k_ref[pl.ds(off, tq), head_cols(j)]
            v = v_ref[pl.ds(off, tq), head_cols(j)]
            s = _dot_nt(qqs[j], k) + slopes[j] * rel
            if masked:
                row = lax.broadcasted_iota(jnp.int32, s.shape, 0)
                row = jnp.where(row >= tq, row - tq, row)
                s = jnp.where(row >= lax.broadcasted_iota(jnp.int32, s.shape, 1), s, NEG_INF)
            m_old = m_ref[j]
            m_new = jnp.maximum(m_old, jnp.max(s, axis=-1, keepdims=True))
            alpha = jnp.exp(m_old - m_new)
            p = jnp.exp(s - jnp.concatenate([m_new] * lane_reps, axis=1))
            l_ref[j] = alpha * l_ref[j] + jnp.sum(p, axis=-1, keepdims=True)
            acc_ref[j] = alpha * acc_ref[j] + jnp.dot(p.astype(BF16), v,
                                                      preferred_element_type=F32)
            m_ref[j] = m_new

    def full_block(kj, carry):
        block(kj, False)
        return carry

    lax.fori_loop(0, qi, full_block, 0)
    block(qi, True)

    for j in range(hp):
        o = acc_ref[j] / l_ref[j]
        o = o[:tq] - lam * o[tq:]
        o_ref[:, head_cols(j)] = (_rms(o) * g_ref[...] * out_scale).astype(o_ref.dtype)


def _prompt_attention(scalars, q, k, v, subln_g, bsz, t, out_scale, tq, hp):
    n, width = q.shape
    nq = t // tq
    assert ATTN_DV == LANES and tq % LANES == 0 and ATTN_HEADS % hp == 0
    wb = hp * ATTN_DV
    return pl.pallas_call(
        functools.partial(_prompt_attn_kernel, tq=tq, hp=hp, out_scale=out_scale),
        grid=(bsz, ATTN_HEADS // hp, nq),
        in_specs=[pl.BlockSpec(memory_space=pltpu.SMEM),
                  pl.BlockSpec((tq, wb), lambda b, h, i: (b * nq + i, h)),
                  pl.BlockSpec((t, wb), lambda b, h, i: (b, h)),
                  pl.BlockSpec((t, wb), lambda b, h, i: (b, h)),
                  pl.BlockSpec((1, ATTN_DV), lambda b, h, i: (0, 0))],
        out_specs=pl.BlockSpec((tq, wb), lambda b, h, i: (b * nq + i, h)),
        out_shape=jax.ShapeDtypeStruct((n, width), BF16),
        scratch_shapes=[pltpu.VMEM((hp, 2 * tq, LANES), F32), pltpu.VMEM((hp, 2 * tq, LANES), F32),
                        pltpu.VMEM((hp, 2 * tq, ATTN_DV), F32)],
        compiler_params=_cparams(("arbitrary", "arbitrary", "arbitrary"),
                                 8 * t * wb * 2 + 16 * hp * tq * tq * 4 + (8 << 20)),
        name="prompt_attn",
    )(scalars, q, k, v, subln_g.reshape(1, ATTN_DV))


ROWS_PER_HEAD = 2 * SUBLANES
HEAD_TILES = ATTN_HEADS // SUBLANES


def _decode_attn_kernel(pt_ref, sc_ref, q_ref, slope_ref, *refs, n_steps, pps, page, past_len,
                        out_scale):
    del pt_ref
    nt = HEAD_TILES
    kc_refs = [refs[i * nt:(i + 1) * nt] for i in range(pps)]
    vc_refs = [refs[(pps + i) * nt:(pps + i + 1) * nt] for i in range(pps)]
    rest = refs[2 * pps * nt:]
    kn_refs, vn_refs = rest[:nt], rest[nt:2 * nt]
    g_ref, o_ref, m_ref, l_ref, acc_ref = rest[2 * nt:]
    p = pl.program_id(1)

    @pl.when(p == 0)
    def _():
        m_ref[...] = jnp.full(m_ref.shape, NEG_INF, F32)
        l_ref[...] = jnp.zeros(l_ref.shape, F32)
        acc_ref[...] = jnp.zeros(acc_ref.shape, F32)

    def head_rows(h):
        return slice(h * ROWS_PER_HEAD, (h + 1) * ROWS_PER_HEAD)

    def head_keys(page_refs, h):
        parts = [tiles[h // SUBLANES].reshape(page * SUBLANES, ATTN_DV)[
            pl.ds(h % SUBLANES, page, stride=SUBLANES), :].astype(BF16) for tiles in page_refs]
        return parts[0] if len(parts) == 1 else jnp.concatenate(parts, axis=0)

    def lane_tile(x, reps):
        return x if reps == 1 else jnp.concatenate([x] * reps, axis=1)

    def process(k_refs, v_refs, base_pos, causal):
        reps = len(k_refs)
        col = lax.broadcasted_iota(jnp.int32, (1, reps * page), 1)
        rel = (col + base_pos).astype(F32)
        s = jnp.concatenate([_dot_nt(q_ref[head_rows(h), :], head_keys(k_refs, h))
                             for h in range(ATTN_HEADS)], axis=0)
        s = s + lane_tile(slope_ref[...], reps) * rel
        if causal:
            t = lax.broadcasted_iota(jnp.int32, s.shape, 0) & (SUBLANES - 1)
            s = jnp.where(lax.broadcasted_iota(jnp.int32, s.shape, 1) <= t, s, NEG_INF)
        m_old = m_ref[...]
        m_new = jnp.maximum(m_old, jnp.max(s, axis=-1, keepdims=True))
        alpha = jnp.exp(m_old - m_new)
        e = jnp.exp(s - lane_tile(m_new, reps))
        l_ref[...] = alpha * l_ref[...] + jnp.sum(e, axis=-1, keepdims=True)
        eb = e.astype(BF16)
        pv = jnp.concatenate([jnp.dot(eb[head_rows(h), :], head_keys(v_refs, h),
                                      preferred_element_type=F32)
                              for h in range(ATTN_HEADS)], axis=0)
        acc_ref[...] = alpha * acc_ref[...] + pv
        m_ref[...] = m_new

    @pl.when(p < n_steps)
    def _():
        process(kc_refs, vc_refs, p * (pps * page) - past_len, False)

    @pl.when(p == n_steps)
    def _():
        process([kn_refs], [vn_refs], 0, True)
        lam = sc_ref[0]
        o = acc_ref[...] / l_ref[...]
        for h in range(ATTN_HEADS):
            r0 = h * ROWS_PER_HEAD
            oh = o[r0:r0 + SUBLANES] - lam * o[r0 + SUBLANES:r0 + ROWS_PER_HEAD]
            o_ref[h * SUBLANES:(h + 1) * SUBLANES, :] = _rms(oh) * g_ref[...] * out_scale


def _decode_attention(page_table, scalars, q_rows, cache_k, cache_v, k_new, v_new, subln_g,
                      page, past_len, out_scale, slopes, pps):
    bsz, n_pages = page_table.shape
    rows = ATTN_HEADS * ROWS_PER_HEAD
    assert cache_k.shape[1:] == (page, HEAD_TILES, SUBLANES, ATTN_DV)
    assert page == ATTN_DV == LANES and n_pages % pps == 0
    page_rows = page * ATTN_HEADS
    n_steps = n_pages // pps
    slope_rows = jnp.broadcast_to(
        jnp.repeat(jnp.asarray(slopes, F32), ROWS_PER_HEAD)[:, None], (rows, page))
    half_block = (None, page, None, SUBLANES, ATTN_DV)

    def cache_spec(slot, tile):
        def index_map(b, p, pt):
            return (pt[b, jnp.minimum(p, n_steps - 1) * pps + slot], 0, tile, 0, 0)
        return pl.BlockSpec(half_block, index_map)

    def new_spec(tile):
        return pl.BlockSpec(half_block, lambda b, p, pt: (b, 0, tile, 0, 0))

    cache_specs = [cache_spec(i, t) for i in range(pps) for t in range(HEAD_TILES)]
    new_specs = [new_spec(t) for t in range(HEAD_TILES)]
    grid_spec = pltpu.PrefetchScalarGridSpec(
        num_scalar_prefetch=1,
        grid=(bsz, n_steps + 1),
        in_specs=[pl.BlockSpec(memory_space=pltpu.SMEM),
                  pl.BlockSpec((None, rows, ATTN_DV), lambda b, p, pt: (b, 0, 0)),
                  pl.BlockSpec((rows, page), lambda b, p, pt: (0, 0))]
                 + cache_specs * 2 + new_specs * 2
                 + [pl.BlockSpec((1, ATTN_DV), lambda b, p, pt: (0, 0))],
        out_specs=pl.BlockSpec((None, ATTN_HEADS * SUBLANES, ATTN_DV), lambda b, p, pt: (b, 0, 0)),
        scratch_shapes=[pltpu.VMEM((rows, page), F32), pltpu.VMEM((rows, page), F32),
                        pltpu.VMEM((rows, ATTN_DV), F32)],
    )
    return pl.pallas_call(
        functools.partial(_decode_attn_kernel, n_steps=n_steps, pps=pps, page=page,
                          past_len=past_len, out_scale=out_scale),
        grid_spec=grid_spec,
        out_shape=jax.ShapeDtypeStruct((bsz, ATTN_HEADS * SUBLANES, ATTN_DV), F32),
        compiler_params=_cparams(("arbitrary", "arbitrary"),
                                 (4 * pps + 4) * page_rows * ATTN_DV * 4 + (12 << 20)),
        name="decode_attn",
    )(page_table, scalars, q_rows, slope_rows,
      *([cache_k] * (pps * HEAD_TILES)), *([cache_v] * (pps * HEAD_TILES)),
      *([k_new] * HEAD_TILES), *([v_new] * HEAD_TILES), subln_g.reshape(1, ATTN_DV))


def _s5_kernel(*refs, tl, chain, emit_y):
    refs = list(refs)
    u_ref, bb_ref, a_ref, h0_ref = refs[:4]
    refs = refs[4:]
    if chain:
        aseg_ref = refs.pop(0)
    if emit_y:
        c_ref, d_ref = refs[:2]
        refs = refs[2:]
        y_ref = refs.pop(0)
    ht_ref = refs.pop(0)
    hs_ref = refs.pop(0)
    if chain:
        init_ref = refs.pop(0)

    n_steps, n_seq, _ = u_ref.shape
    sb = STATE_BLOCK
    ar = a_ref[0:1, :]
    ai = a_ref[1:2, :]

    if chain:
        seg_r = aseg_ref[0:1, :]
        seg_i = aseg_ref[1:2, :]
        for b in range(n_seq // PROMPT_SEGMENTS):
            r = jnp.zeros((1, sb), F32)
            i = jnp.zeros((1, sb), F32)
            for j in range(PROMPT_SEGMENTS):
                row = b * PROMPT_SEGMENTS + j
                init_ref[0, row:row + 1, :] = r
                init_ref[1, row:row + 1, :] = i
                er = h0_ref[0, row:row + 1, :]
                ei = h0_ref[1, row:row + 1, :]
                r, i = seg_r * r - seg_i * i + er, seg_r * i + seg_i * r + ei
        h_init = (init_ref[0], init_ref[1])
    else:
        h_init = (h0_ref[0], h0_ref[1])

    def chunk(c, carry):
        t0 = pl.multiple_of(c * tl, tl)
        x = u_ref[pl.ds(t0, tl)].reshape(tl * n_seq, LANES)
        hs_ref[...] = jnp.dot(x.astype(BF16), bb_ref[...], preferred_element_type=F32)

        def step(t, hc):
            hr, hi = hc
            r0 = pl.multiple_of(t * n_seq, n_seq)
            bur = hs_ref[pl.ds(r0, n_seq), 0:sb]
            bui = hs_ref[pl.ds(r0, n_seq), sb:2 * sb]
            nr = ar * hr - ai * hi + bur
            ni = ar * hi + ai * hr + bui
            if emit_y:
                hs_ref[pl.ds(r0, n_seq), 0:sb] = nr
                hs_ref[pl.ds(r0, n_seq), sb:2 * sb] = ni
            return nr, ni

        carry = lax.fori_loop(0, tl, step, carry, unroll=True)
        if emit_y:
            y = jnp.dot(hs_ref[...].astype(BF16), c_ref[...], preferred_element_type=F32)
            y = jax.nn.gelu(y + d_ref[...] * x)
            y_ref[pl.ds(t0, tl)] = y.reshape(tl, n_seq, LANES)
        return carry

    hr, hi = lax.fori_loop(0, n_steps // tl, chunk, h_init)
    ht_ref[0] = hr
    ht_ref[1] = hi


def _s5_scan(u_tm, bb_blk, a_blk, h0, aseg_blk, c_blk, d_blk, *, tl, chain, emit_y):
    n_steps, n_seq, width = u_tm.shape
    nblk = width // LANES
    sb = STATE_BLOCK
    in_specs = [pl.BlockSpec((n_steps, n_seq, LANES), lambda j: (0, 0, j)),
                pl.BlockSpec((None, LANES, 2 * sb), lambda j: (j, 0, 0)),
                pl.BlockSpec((None, 2, sb), lambda j: (j, 0, 0)),
                pl.BlockSpec((2, n_seq, sb), lambda j: (0, 0, j))]
    args = [u_tm, bb_blk, a_blk, h0]
    if chain:
        in_specs.append(pl.BlockSpec((None, 2, sb), lambda j: (j, 0, 0)))
        args.append(aseg_blk)
    out_specs = []
    out_shape = []
    if emit_y:
        in_specs += [pl.BlockSpec((None, 2 * sb, LANES), lambda j: (j, 0, 0)),
                     pl.BlockSpec((None, 1, LANES), lambda j: (j, 0, 0))]
        args += [c_blk, d_blk]
        out_specs.append(pl.BlockSpec((n_steps, n_seq, LANES), lambda j: (0, 0, j)))
        out_shape.append(jax.ShapeDtypeStruct((n_steps, n_seq, width), F32))
    out_specs.append(pl.BlockSpec((2, n_seq, sb), lambda j: (0, 0, j)))
    out_shape.append(jax.ShapeDtypeStruct((2, n_seq, nblk * sb), F32))
    scratch = [pltpu.VMEM((tl * n_seq, 2 * sb), F32)]
    if chain:
        scratch.append(pltpu.VMEM((2, n_seq, sb), F32))
    io_bytes = 2 * (1 + int(emit_y)) * n_steps * n_seq * LANES * 4
    return pl.pallas_call(
        functools.partial(_s5_kernel, tl=tl, chain=chain, emit_y=emit_y),
        grid=(nblk,),
        in_specs=in_specs, out_specs=out_specs, out_shape=out_shape,
        scratch_shapes=scratch,
        compiler_params=_cparams(("arbitrary",),
                                 io_bytes + 3 * tl * n_seq * 2 * sb * 4 + (12 << 20)),
        name="s5_scan",
    )(*args)


def _glu_kernel(y_ref, w_ref, o_ref):
    y = y_ref[...]
    z = jnp.dot(y.astype(BF16), w_ref[...], preferred_element_type=F32)
    o_ref[...] = (y * jax.nn.sigmoid(z)).astype(o_ref.dtype)


def _glu(y, w, tm):
    m, k = y.shape
    return pl.pallas_call(
        _glu_kernel,
        grid=(m // tm,),
        in_specs=[pl.BlockSpec((tm, k), lambda i: (i, 0)),
                  pl.BlockSpec((k, k), lambda i: (0, 0))],
        out_specs=pl.BlockSpec((tm, k), lambda i: (i, 0)),
        out_shape=jax.ShapeDtypeStruct((m, k), BF16),
        compiler_params=_cparams(("arbitrary",), 2 * k * k * 2 + 6 * tm * k * 4 + (8 << 20)),
        name="glu",
    )(y, w)


def _out_proj_kernel(a1_ref, a2_ref, w1_ref, w2_ref, x_ref, g_ref, o_ref):
    acc = jnp.dot(a1_ref[...], w1_ref[...], preferred_element_type=F32)
    acc = acc + jnp.dot(a2_ref[...], w2_ref[...], preferred_element_type=F32)
    o_ref[...] = x_ref[...] + g_ref[...] * acc


def _out_proj(a1, a2, w, x, gate, tm, tn):
    bsz, t, d = x.shape
    k1, k2 = a1.shape[-1], a2.shape[-1]
    a1 = a1.reshape(bsz, t, k1)
    a2 = a2.reshape(bsz, t, k2)
    if gate.shape[1] == 1:
        g_spec = pl.BlockSpec((None, 1, tn), lambda b, i, j: (b, 0, j))
    else:
        g_spec = pl.BlockSpec((None, tm, tn), lambda b, i, j: (b, i, j))
    return pl.pallas_call(
        _out_proj_kernel,
        grid=(bsz, t // tm, d // tn),
        in_specs=[pl.BlockSpec((None, tm, k1), lambda b, i, j: (b, i, 0)),
                  pl.BlockSpec((None, tm, k2), lambda b, i, j: (b, i, 0)),
                  pl.BlockSpec((k1, tn), lambda b, i, j: (0, j)),
                  pl.BlockSpec((k2, tn), lambda b, i, j: (k1 // k2, j)),
                  pl.BlockSpec((None, tm, tn), lambda b, i, j: (b, i, j)),
                  g_spec],
        out_specs=pl.BlockSpec((None, tm, tn), lambda b, i, j: (b, i, j)),
        out_shape=jax.ShapeDtypeStruct((bsz, t, d), F32),
        compiler_params=_cparams(("arbitrary", "arbitrary", "arbitrary"),
                                 2 * (tm * (k1 + k2) * 2 + (k1 + k2) * tn * 2 + 3 * tm * tn * 4)
                                 + (8 << 20)),
        name="out_proj",
    )(a1, a2, w, w, x, gate)


def _top_rows(s, k):
    n_rows = s.shape[0]
    row = lax.broadcasted_iota(jnp.int32, s.shape, 0).astype(F32)
    k_iota = lax.broadcasted_iota(jnp.int32, (k, s.shape[1]), 0)
    rank = jnp.full(s.shape, NOT_SELECTED, F32)
    val_mat = jnp.zeros((k, s.shape[1]), F32)
    vals, idxs = [], []
    for kk in range(k):
        mx = jnp.max(s, axis=0, keepdims=True)
        idx = jnp.min(jnp.where(s == mx, row, float(n_rows)), axis=0, keepdims=True)
        hit = row == idx
        rank = jnp.where(hit, float(kk), rank)
        s = jnp.where(hit, -jnp.inf, s)
        val_mat = jnp.where(k_iota == kk, mx, val_mat)
        vals.append(mx)
        idxs.append(idx)
    return vals, val_mat, idxs, rank


def _peer_route_kernel(h_ref, wq_ref, keys_ref, p0_ref, th_ref, p1_ref, r1_ref):
    q_t = _dot_nt(wq_ref[...], h_ref[...]).astype(BF16)
    s0 = jnp.dot(keys_ref[0], q_t[:PEER_DHALF], preferred_element_type=F32)
    s1 = jnp.dot(keys_ref[1], q_t[PEER_DHALF:], preferred_element_type=F32)
    v0, _, _, rank0 = _top_rows(s0, PEER_TOPK)
    v1, sv1, _, rank1 = _top_rows(s1, PEER_TOPK)
    tm = s0.shape[1]
    n_b = [PEER_TOPK // (a + 1) for a in range(PEER_TOPK)]
    n_cand = sum(n_b)
    pad = (-n_cand) % SUBLANES
    parts = [v0[a] + sv1[:n_b[a]] for a in range(PEER_TOPK)]
    if pad:
        parts.append(jnp.full((pad, tm), -jnp.inf, F32))
    top_s, _, _, cand_rank = _top_rows(jnp.concatenate(parts, axis=0), PEER_TOPK)
    taken = jnp.where(cand_rank < float(PEER_TOPK), 1.0, 0.0)
    z = jnp.zeros((1, tm), F32)
    for kk in range(PEER_TOPK):
        z = z + jnp.exp(top_s[kk] - top_s[0])
    th = jnp.zeros(s0.shape, F32)
    start = 0
    for a in range(PEER_TOPK):
        n_sel = jnp.sum(taken[start:start + n_b[a]], axis=0, keepdims=True)
        th = jnp.where(rank0 == float(a), n_sel, th)
        start += n_b[a]
    p0_ref[...] = jnp.exp(s0 - v0[0])
    th_ref[...] = th
    p1_ref[...] = jnp.exp(s1 - v1[0]) / z
    r1_ref[...] = rank1


def _peer_route(h2, wq_t, keys, tm):
    n, d = h2.shape
    dk = 2 * PEER_DHALF
    spec_out = pl.BlockSpec((None, PEER_NKEYS, tm), lambda i, h: (h, 0, i))
    shape_out = jax.ShapeDtypeStruct((PEER_HEADS, PEER_NKEYS, n), F32)
    return pl.pallas_call(
        _peer_route_kernel,
        grid=(n // tm, PEER_HEADS),
        in_specs=[pl.BlockSpec((tm, d), lambda i, h: (i, 0)),
                  pl.BlockSpec((dk, d), lambda i, h: (h, 0)),
                  pl.BlockSpec((None, 2, PEER_NKEYS, PEER_DHALF), lambda i, h: (h, 0, 0, 0))],
        out_specs=[spec_out] * 4,
        out_shape=[shape_out] * 4,
        compiler_params=_cparams(("arbitrary", "arbitrary"),
                                 2 * (tm * d * 2 + dk * d * 2) + 40 * PEER_NKEYS * tm * 4 + (8 << 20)),
        name="peer_route",
    )(h2, wq_t, keys)


def _peer_dense_kernel(h_ref, u_ref, vt_ref, p0_ref, th_ref, p1_ref, r1_ref, o_ref, w_ref, *, te):
    e = pl.program_id(1)

    @pl.when(e == 0)
    def _():
        o_ref[...] = jnp.zeros(o_ref.shape, F32)

    for ib in range(te // PEER_NKEYS):
        i0 = e * (te // PEER_NKEYS) + ib
        th_rows = [th_ref[h, pl.ds(i0, 1), :] for h in range(PEER_HEADS)]
        p0_rows = [p0_ref[h, pl.ds(i0, 1), :] for h in range(PEER_HEADS)]
        for c in range(w_ref.shape[1] // LANES):
            tok = slice(c * LANES, (c + 1) * LANES)
            w = jnp.zeros((PEER_NKEYS, LANES), F32)
            for h in range(PEER_HEADS):
                w = w + jnp.where(r1_ref[h, :, tok] < th_rows[h][:, tok],
                                  p1_ref[h, :, tok] * p0_rows[h][:, tok], 0.0)
            w_ref[ib * PEER_NKEYS:(ib + 1) * PEER_NKEYS, tok] = w

    act = jax.nn.gelu(_dot_nt(u_ref[...], h_ref[...]))
    coef = (w_ref[...] * act).astype(BF16)
    o_ref[...] += jnp.dot(vt_ref[...], coef, preferred_element_type=F32)


def _peer_dense(h2, u, v_t, sel, tm, te):
    n, d = h2.shape
    n_exp = u.shape[0]
    once = pl.Buffered(1)
    sel_spec = pl.BlockSpec((PEER_HEADS, PEER_NKEYS, tm), lambda i, e: (0, 0, i),
                            pipeline_mode=once)
    return pl.pallas_call(
        functools.partial(_peer_dense_kernel, te=te),
        grid=(n // tm, n_exp // te),
        in_specs=[pl.BlockSpec((tm, d), lambda i, e: (i, 0), pipeline_mode=once),
                  pl.BlockSpec((te, d), lambda i, e: (e, 0)),
                  pl.BlockSpec((d, te), lambda i, e: (0, e))] + [sel_spec] * 4,
        out_specs=pl.BlockSpec((d, tm), lambda i, e: (0, i), pipeline_mode=once),
        out_shape=jax.ShapeDtypeStruct((d, n), F32),
        scratch_shapes=[pltpu.VMEM((te, tm), F32)],
        compiler_params=_cparams(
            ("arbitrary", "arbitrary"),
            tm * d * 2 + 2 * 2 * te * d * 2 + 4 * PEER_HEADS * PEER_NKEYS * tm * 4 + d * tm * 4
            + 8 * te * tm * 4 + (6 << 20)),
        name="peer_dense",
    )(h2, u, v_t, *sel)


def _peer_out_kernel(ot_ref, x_ref, g_ref, fg_ref, y_ref):
    x2 = x_ref[...] + g_ref[...] * ot_ref[...].T
    y_ref[...] = _rms(x2) * fg_ref[...]


def _peer_out(o_t, x, gate, final_g, tm):
    bsz, t, d = x.shape
    nt = t // tm
    return pl.pallas_call(
        _peer_out_kernel,
        grid=(bsz, nt),
        in_specs=[pl.BlockSpec((d, tm), lambda b, i: (0, b * nt + i)),
                  pl.BlockSpec((None, tm, d), lambda b, i: (b, i, 0)),
                  _mod_spec(gate, tm, d),
                  pl.BlockSpec((1, d), lambda b, i: (0, 0))],
        out_specs=pl.BlockSpec((None, tm, d), lambda b, i: (b, i, 0)),
        out_shape=jax.ShapeDtypeStruct((bsz, t, d), F32),
        compiler_params=_cparams(("arbitrary", "arbitrary"), 10 * tm * d * 4 + (8 << 20)),
        name="peer_out",
    )(o_t, x, gate, final_g.reshape(1, d))


def _alibi_slopes(n):
    return tuple(float(np.float32(2.0 ** (-8.0 * (h + 1) / n))) for h in range(n))


def _s5_params(lam_re, lam_im, log_dt, b_re, b_im, c_re, c_im, d_skip, seg_len):
    dt = jnp.exp(log_dt.astype(F32))[:, None]
    lr, li = lam_re.astype(F32), lam_im.astype(F32)
    mag = jnp.exp(lr * dt)
    a_re, a_im = mag * jnp.cos(li * dt), mag * jnp.sin(li * dt)
    den = lr * lr + li * li
    nr, ni = a_re - 1.0, a_im
    f_re, f_im = (nr * lr + ni * li) / den, (ni * lr - nr * li) / den
    br, bi = b_re.astype(F32), b_im.astype(F32)
    bb_re = f_re[..., None] * br - f_im[..., None] * bi
    bb_im = f_re[..., None] * bi + f_im[..., None] * br
    mag_s = jnp.exp(lr * dt * seg_len)
    s_re, s_im = mag_s * jnp.cos(li * dt * seg_len), mag_s * jnp.sin(li * dt * seg_len)

    g = lam_re.shape[0]
    nblk = g // GROUPS_PER_BLOCK
    eye = jnp.eye(GROUPS_PER_BLOCK, dtype=F32)

    def in_blocks(w):
        w = w.reshape(nblk, GROUPS_PER_BLOCK, SSM_STATE, SSM_GROUP)
        return jnp.einsum('jgnc,gk->jgckn', w, eye).reshape(nblk, LANES, STATE_BLOCK)

    def out_blocks(w):
        w = w.reshape(nblk, GROUPS_PER_BLOCK, SSM_GROUP, SSM_STATE)
        return jnp.einsum('jgcn,gk->jgnkc', w, eye).reshape(nblk, STATE_BLOCK, LANES)

    bb_blk = jnp.concatenate([in_blocks(bb_re), in_blocks(bb_im)], axis=2).astype(BF16)
    c_blk = jnp.concatenate([out_blocks(c_re.astype(F32)), -out_blocks(c_im.astype(F32))],
                            axis=1).astype(BF16)
    a_blk = jnp.stack([a_re.reshape(nblk, STATE_BLOCK), a_im.reshape(nblk, STATE_BLOCK)], axis=1)
    aseg_blk = jnp.stack([s_re.reshape(nblk, STATE_BLOCK), s_im.reshape(nblk, STATE_BLOCK)], axis=1)
    d_blk = d_skip.astype(F32).reshape(nblk, 1, LANES)
    return bb_blk, a_blk, aseg_blk, c_blk, d_blk


def _pick(n, pref):
    t = min(n, pref)
    while n % t:
        t //= 2
    return t


def kernel(x_prompt, x_sample, cache_k, cache_v, page_table, state_ssm_re, state_ssm_im,
           c_prompt, c_sample, w_ada, b_ada, norm1_g, norm2_g, w_in,
           lam_q1, lam_k1, lam_q2, lam_k2, subln_g,
           ssm_lam_re, ssm_lam_im, ssm_log_dt, ssm_b_re, ssm_b_im, ssm_c_re, ssm_c_im, ssm_d,
           w_glu, w_out, peer_wq, peer_sub_keys, peer_u, peer_v, final_g):
    depth = w_ada.shape[0]
    assert depth == 1, "kernel() fuses the final norm into the layer and supports DEPTH == 1"
    bsz, seq, d_model = x_prompt.shape
    dec_b, dec_t, _ = x_sample.shape
    n_p, n_s = bsz * seq, dec_b * dec_t
    n_pages, page = page_table.shape[1], cache_k.shape[2]
    past_len = n_pages * page
    attn_w = ATTN_HEADS * ATTN_DV
    ssm_w = w_in.shape[2] - 3 * attn_w
    n_groups = ssm_w // SSM_GROUP
    slopes = _alibi_slopes(ATTN_HEADS)
    seg_len = seq // PROMPT_SEGMENTS
    n_seq_p = bsz * PROMPT_SEGMENTS

    xp = x_prompt
    xs = x_sample.reshape(1, n_s, d_model)
    outs = [[] for _ in range(8)]

    for l in range(depth):
        lam_init = 0.8 - 0.6 * math.exp(-0.3 * l)
        out_scale = 1.0 - lam_init
        lam = (jnp.exp(jnp.sum(lam_q1[l].astype(F32) * lam_k1[l].astype(F32)))
               - jnp.exp(jnp.sum(lam_q2[l].astype(F32) * lam_k2[l].astype(F32))) + lam_init)
        scalars = jnp.concatenate([lam.reshape(1), jnp.asarray(slopes, F32)])

        c_all = jnp.concatenate([c_prompt, c_sample], axis=0)
        pad = (-c_all.shape[0]) % SUBLANES
        c_all = jnp.pad(c_all, ((0, pad), (0, 0)))
        mod = _adaln(c_all, w_ada[l], b_ada[l])
        mod_p = [m[:, None, :] for m in jnp.split(mod[:bsz], N_MOD, axis=-1)]
        mod_s = [jnp.repeat(m, dec_t, axis=0)[None] for m in
                 jnp.split(mod[bsz:bsz + dec_b], N_MOD, axis=-1)]

        w_in_b = w_in[l].astype(BF16)
        w_glu_b = w_glu[l].astype(BF16)
        w_out_b = w_out[l].astype(BF16)
        wq_t = peer_wq[l].T.astype(BF16)
        keys_b = peer_sub_keys[l].astype(BF16)
        u_b = peer_u[l].astype(BF16)
        v_t = peer_v[l].T.astype(BF16)
        bb_blk, a_blk, aseg_blk, c_blk, d_blk = _s5_params(
            ssm_lam_re[l], ssm_lam_im[l], ssm_log_dt[l], ssm_b_re[l], ssm_b_im[l],
            ssm_c_re[l], ssm_c_im[l], ssm_d[l], seg_len)

        def front(x, mods, tm):
            sh1, sc1 = mods[0], mods[1]
            h = _norm_mod(x, norm1_g[l], sh1, sc1, tm).reshape(-1, d_model)
            tmm = _pick(h.shape[0], 1024)
            (q_b, q_f) = _matmul_cols(h, w_in_b, 0, attn_w, (BF16, F32), tmm, 512)
            (k_f, k_b) = _matmul_cols(h, w_in_b, attn_w, attn_w, (F32, BF16), tmm, 512)
            (v_f, v_b) = _matmul_cols(h, w_in_b, 2 * attn_w, attn_w, (F32, BF16), tmm, 512)
            (u_f,) = _matmul_cols(h, w_in_b, 3 * attn_w, ssm_w, (F32,), tmm, 512)
            return q_b, q_f, k_f, k_b, v_f, v_b, u_f

        def back(x, mods, o_attn, o_ssm, tm):
            g1, sh2, sc2, g2 = mods[2], mods[3], mods[4], mods[5]
            n = x.shape[0] * x.shape[1]
            x1 = _out_proj(o_attn, o_ssm, w_out_b, x, g1, _pick(x.shape[1], 1024), 512)
            h2 = _norm_mod(x1, norm2_g[l], sh2, sc2, tm).reshape(n, d_model)
            tmp = _pick(n, 512)
            sel = _peer_route(h2, wq_t, keys_b, tmp)
            o_t = _peer_dense(h2, u_b, v_t, sel, tmp, 512)
            return _peer_out(o_t, x1, g2, final_g, _pick(x.shape[1], 256))

        q_b, _, k_f, k_b, v_f, v_b, u_f = front(xp, mod_p, 256)
        o_attn_p = _prompt_attention(scalars, q_b, k_b, v_b, subln_g[l], bsz, seq, out_scale,
                                     256, 4)
        u_tm = u_f.reshape(bsz, PROMPT_SEGMENTS, seg_len, ssm_w).transpose(2, 0, 1, 3)
        u_tm = u_tm.reshape(seg_len, n_seq_p, ssm_w)
        zeros_p = jnp.zeros((2, n_seq_p, n_groups * SSM_STATE), F32)
        (seg_end,) = _s5_scan(u_tm, bb_blk, a_blk, zeros_p, None, None, None,
                              tl=16, chain=False, emit_y=False)
        y_tm, h_end = _s5_scan(u_tm, bb_blk, a_blk, seg_end, aseg_blk, c_blk, d_blk,
                               tl=16, chain=True, emit_y=True)
        y_p = y_tm.reshape(seg_len, bsz, PROMPT_SEGMENTS, ssm_w).transpose(1, 2, 0, 3)
        o_ssm_p = _glu(y_p.reshape(n_p, ssm_w), w_glu_b, 256)
        y_prompt_l = back(xp, mod_p, o_attn_p, o_ssm_p, 256)
        h_end = h_end.reshape(2, bsz, PROMPT_SEGMENTS, n_groups, SSM_STATE)[:, :, -1]
        outs[0].append(k_f.reshape(bsz, seq, ATTN_HEADS, ATTN_DV))
        outs[1].append(v_f.reshape(bsz, seq, ATTN_HEADS, ATTN_DV))
        outs[4].append(h_end[0])
        outs[5].append(h_end[1])

        _, q_f, k_f, _, v_f, _, u_f = front(xs, mod_s, n_s)
        qh = q_f.reshape(dec_b, dec_t, ATTN_HEADS, ATTN_DV).transpose(0, 2, 1, 3)
        lane = jnp.arange(ATTN_DV) < ATTN_DH
        qh = jnp.stack([jnp.where(lane, qh, 0.0), jnp.where(lane, 0.0, qh)], axis=2)
        qh = jnp.pad(qh, ((0, 0), (0, 0), (0, 0), (0, SUBLANES - dec_t), (0, 0)))
        q_rows = (qh * (ATTN_DH ** -0.5)).astype(BF16).reshape(
            dec_b, ATTN_HEADS * ROWS_PER_HEAD, ATTN_DV)
        kv_view = (-1, page, HEAD_TILES, SUBLANES, ATTN_DV)
        new_pad = ((0, 0), (0, page - dec_t), (0, 0), (0, 0), (0, 0))
        k_new = jnp.pad(k_f.reshape((dec_b, dec_t) + kv_view[2:]), new_pad)
        v_new = jnp.pad(v_f.reshape((dec_b, dec_t) + kv_view[2:]), new_pad)
        o_rows = _decode_attention(
            page_table, scalars, q_rows, cache_k.reshape(kv_view), cache_v.reshape(kv_view),
            k_new, v_new, subln_g[l], page, past_len, out_scale, slopes, 4)
        o_attn_s = o_rows.reshape(dec_b, ATTN_HEADS, SUBLANES, ATTN_DV)[:, :, :dec_t]
        o_attn_s = o_attn_s.transpose(0, 2, 1, 3).reshape(n_s, attn_w).astype(BF16)
        u_tm = u_f.reshape(dec_b, dec_t, ssm_w).transpose(1, 0, 2)
        h0 = jnp.stack([state_ssm_re[l].reshape(dec_b, -1), state_ssm_im[l].reshape(dec_b, -1)])
        y_tm, h_end = _s5_scan(u_tm, bb_blk, a_blk, h0.astype(F32), None, c_blk, d_blk,
                               tl=dec_t, chain=False, emit_y=True)
        o_ssm_s = _glu(y_tm.transpose(1, 0, 2).reshape(n_s, ssm_w), w_glu_b, n_s)
        y_sample_l = back(xs, mod_s, o_attn_s, o_ssm_s, n_s)
        outs[2].append(k_f.reshape(dec_b, dec_t, ATTN_HEADS, ATTN_DV))
        outs[3].append(v_f.reshape(dec_b, dec_t, ATTN_HEADS, ATTN_DV))
        outs[6].append(h_end[0].reshape(dec_b, n_groups, SSM_STATE))
        outs[7].append(h_end[1].reshape(dec_b, n_groups, SSM_STATE))

    y_prompt = y_prompt_l
    y_sample = y_sample_l.reshape(dec_b, dec_t, d_model)
    return (y_prompt, y_sample) + tuple(jnp.stack(o) for o in outs)
```

```python
import functools
import math

import numpy as np
import jax
import jax.numpy as jnp
from jax import lax
from jax.experimental import pallas as pl
from jax.experimental.pallas import tpu as pltpu

F32 = jnp.float32
BF16 = jnp.bfloat16

V7X_VMEM_BYTES = 64 * 1024 * 1024
LANES = 128
SUBLANES = 8

EPS = 1e-6
NEG_INF = -1e30

ATTN_HEADS = 16
ATTN_DH = 64
ATTN_DV = 2 * ATTN_DH
SSM_GROUP = 16
SSM_STATE = 64
GROUPS_PER_BLOCK = LANES // SSM_GROUP
STATE_BLOCK = GROUPS_PER_BLOCK * SSM_STATE
PROMPT_SEGMENTS = 8
PEER_HEADS = 8
PEER_NKEYS = 128
PEER_DHALF = 128
PEER_TOPK = 16
N_MOD = 6
NOT_SELECTED = 99.0


def _cparams(semantics, vmem_bytes):
    return pltpu.CompilerParams(dimension_semantics=semantics,
                                vmem_limit_bytes=min(int(vmem_bytes), V7X_VMEM_BYTES - (4 << 20)))


def _dot_nt(a, b):
    return lax.dot_general(a, b, (((1,), (1,)), ((), ())), preferred_element_type=F32)


def _rms(x):
    return x * lax.rsqrt(jnp.mean(x * x, axis=-1, keepdims=True) + EPS)


def _adaln_kernel(c_ref, w_ref, b_ref, o_ref):
    a = jax.nn.silu(c_ref[...])
    o_ref[...] = jnp.dot(a.astype(BF16), w_ref[...].astype(BF16),
                         preferred_element_type=F32) + b_ref[...]


def _adaln(c, w, b, tn=1024):
    m, k = c.shape
    n = w.shape[1]
    return pl.pallas_call(
        _adaln_kernel,
        grid=(n // tn,),
        in_specs=[pl.BlockSpec((m, k), lambda j: (0, 0)),
                  pl.BlockSpec((k, tn), lambda j: (0, j)),
                  pl.BlockSpec((1, tn), lambda j: (0, j))],
        out_specs=pl.BlockSpec((m, tn), lambda j: (0, j)),
        out_shape=jax.ShapeDtypeStruct((m, n), F32),
        compiler_params=_cparams(("arbitrary",), 2 * k * tn * 4 + (8 << 20)),
        name="adaln",
    )(c, w, b.reshape(1, n))


def _norm_mod_kernel(x_ref, g_ref, sh_ref, sc_ref, o_ref):
    y = _rms(x_ref[...]) * g_ref[...]
    o_ref[...] = (y * (1.0 + sc_ref[...]) + sh_ref[...]).astype(o_ref.dtype)


def _mod_spec(mod, tm, d):
    if mod.shape[1] == 1:
        return pl.BlockSpec((None, 1, d), lambda b, i: (b, 0, 0))
    return pl.BlockSpec((None, tm, d), lambda b, i: (b, i, 0))


def _norm_mod(x, g, shift, scale, tm):
    bsz, t, d = x.shape
    return pl.pallas_call(
        _norm_mod_kernel,
        grid=(bsz, t // tm),
        in_specs=[pl.BlockSpec((None, tm, d), lambda b, i: (b, i, 0)),
                  pl.BlockSpec((1, d), lambda b, i: (0, 0)),
                  _mod_spec(shift, tm, d), _mod_spec(scale, tm, d)],
        out_specs=pl.BlockSpec((None, tm, d), lambda b, i: (b, i, 0)),
        out_shape=jax.ShapeDtypeStruct((bsz, t, d), BF16),
        compiler_params=_cparams(("arbitrary", "arbitrary"), 8 * tm * d * 4 + (8 << 20)),
        name="norm_mod",
    )(x, g.reshape(1, d), shift, scale)


def _mm_kernel(a_ref, b_ref, *o_refs):
    acc = jnp.dot(a_ref[...], b_ref[...], preferred_element_type=F32)
    for o_ref in o_refs:
        o_ref[...] = acc.astype(o_ref.dtype)


def _matmul_cols(a, b, col0, ncols, out_dtypes, tm, tn):
    m, k = a.shape
    joff = col0 // tn
    out_bytes = sum(jnp.dtype(dt).itemsize for dt in out_dtypes)
    return pl.pallas_call(
        _mm_kernel,
        grid=(m // tm, ncols // tn),
        in_specs=[pl.BlockSpec((tm, k), lambda i, j: (i, 0)),
                  pl.BlockSpec((k, tn), lambda i, j: (0, j + joff))],
        out_specs=[pl.BlockSpec((tm, tn), lambda i, j: (i, j)) for _ in out_dtypes],
        out_shape=[jax.ShapeDtypeStruct((m, ncols), dt) for dt in out_dtypes],
        compiler_params=_cparams(("arbitrary", "arbitrary"),
                                 2 * (tm * k * 2 + k * tn * 2 + tm * tn * out_bytes)
                                 + tm * tn * 4 + (8 << 20)),
        name="matmul_cols",
    )(a, b)


def _prompt_attn_kernel(sc_ref, q_ref, k_ref, v_ref, g_ref, o_ref, m_ref, l_ref, acc_ref,
                        *, tq, hp, out_scale):
    hg = pl.program_id(1)
    qi = pl.program_id(2)
    lam = sc_ref[0]
    slopes = [sc_ref[1 + hg * hp + j] for j in range(hp)]

    def head_cols(j):
        return slice(j * ATTN_DV, (j + 1) * ATTN_DV)

    qqs = []
    for j in range(hp):
        q = q_ref[:, head_cols(j)] * jnp.asarray(ATTN_DH ** -0.5, BF16)
        lane = lax.broadcasted_iota(jnp.int32, q.shape, 1)
        zero = jnp.zeros_like(q)
        qqs.append(jnp.concatenate([jnp.where(lane < ATTN_DH, q, zero),
                                    jnp.where(lane >= ATTN_DH, q, zero)], axis=0))

    m_ref[...] = jnp.full(m_ref.shape, NEG_INF, F32)
    l_ref[...] = jnp.zeros(l_ref.shape, F32)
    acc_ref[...] = jnp.zeros(acc_ref.shape, F32)
    col = lax.broadcasted_iota(jnp.int32, (1, tq), 1)
    lane_reps = tq // LANES

    def block(kj, masked):
        off = pl.multiple_of(kj * tq, tq)
        rel = (col + (kj - qi) * tq).astype(F32)
        for j in range(hp):
            k = k_ref[pl.ds(off, tq), head_cols(j)]
            v = v_ref[pl.ds(off, tq), head_cols(j)]
            s = _dot_nt(qqs[j], k) + slopes[j] * rel
            if masked:
                row = lax.broadcasted_iota(jnp.int32, s.shape, 0)
                row = jnp.where(row >= tq, row - tq, row)
                s = jnp.where(row >= lax.broadcasted_iota(jnp.int32, s.shape, 1), s, NEG_INF)
            m_old = m_ref[j]
            m_new = jnp.maximum(m_old, jnp.max(s, axis=-1, keepdims=True))
            alpha = jnp.exp(m_old - m_new)
            p = jnp.exp(s - jnp.concatenate([m_new] * lane_reps, axis=1))
            l_ref[j] = alpha * l_ref[j] + jnp.sum(p, axis=-1, keepdims=True)
            acc_ref[j] = alpha * acc_ref[j] + jnp.dot(p.astype(BF16), v,
                                                      preferred_element_type=F32)
            m_ref[j] = m_new

    def full_block(kj, carry):
        block(kj, False)
        return carry

    lax.fori_loop(0, qi, full_block, 0)
    block(qi, True)

    for j in range(hp):
        o = acc_ref[j] / l_ref[j]
        o = o[:tq] - lam * o[tq:]
        o_ref[:, head_cols(j)] = (_rms(o) * g_ref[...] * out_scale).astype(o_ref.dtype)


def _prompt_attention(scalars, q, k, v, subln_g, bsz, t, out_scale, tq, hp):
    n, width = q.shape
    nq = t // tq
    assert ATTN_DV == LANES and tq % LANES == 0 and ATTN_HEADS % hp == 0
    wb = hp * ATTN_DV
    return pl.pallas_call(
        functools.partial(_prompt_attn_kernel, tq=tq, hp=hp, out_scale=out_scale),
        grid=(bsz, ATTN_HEADS // hp, nq),
        in_specs=[pl.BlockSpec(memory_space=pltpu.SMEM),
                  pl.BlockSpec((tq, wb), lambda b, h, i: (b * nq + i, h)),
                  pl.BlockSpec((t, wb), lambda b, h, i: (b, h)),
                  pl.BlockSpec((t, wb), lambda b, h, i: (b, h)),
                  pl.BlockSpec((1, ATTN_DV), lambda b, h, i: (0, 0))],
        out_specs=pl.BlockSpec((tq, wb), lambda b, h, i: (b * nq + i, h)),
        out_shape=jax.ShapeDtypeStruct((n, width), BF16),
        scratch_shapes=[pltpu.VMEM((hp, 2 * tq, LANES), F32), pltpu.VMEM((hp, 2 * tq, LANES), F32),
                        pltpu.VMEM((hp, 2 * tq, ATTN_DV), F32)],
        compiler_params=_cparams(("arbitrary", "arbitrary", "arbitrary"),
                                 8 * t * wb * 2 + 16 * hp * tq * tq * 4 + (8 << 20)),
        name="prompt_attn",
    )(scalars, q, k, v, subln_g.reshape(1, ATTN_DV))


ROWS_PER_HEAD = 2 * SUBLANES
HEAD_TILES = ATTN_HEADS // SUBLANES


def _decode_attn_kernel(pt_ref, sc_ref, q_ref, slope_ref, *refs, n_steps, pps, page, past_len,
                        out_scale):
    del pt_ref
    nt = HEAD_TILES
    kc_refs = [refs[i * nt:(i + 1) * nt] for i in range(pps)]
    vc_refs = [refs[(pps + i) * nt:(pps + i + 1) * nt] for i in range(pps)]
    rest = refs[2 * pps * nt:]
    kn_refs, vn_refs = rest[:nt], rest[nt:2 * nt]
    g_ref, o_ref, m_ref, l_ref, acc_ref = rest[2 * nt:]
    p = pl.program_id(1)

    @pl.when(p == 0)
    def _():
        m_ref[...] = jnp.full(m_ref.shape, NEG_INF, F32)
        l_ref[...] = jnp.zeros(l_ref.shape, F32)
        acc_ref[...] = jnp.zeros(acc_ref.shape, F32)

    def head_rows(h):
        return slice(h * ROWS_PER_HEAD, (h + 1) * ROWS_PER_HEAD)

    def head_keys(page_refs, h):
        parts = [tiles[h // SUBLANES].reshape(page * SUBLANES, ATTN_DV)[
            pl.ds(h % SUBLANES, page, stride=SUBLANES), :].astype(BF16) for tiles in page_refs]
        return parts[0] if len(parts) == 1 else jnp.concatenate(parts, axis=0)

    def lane_tile(x, reps):
        return x if reps == 1 else jnp.concatenate([x] * reps, axis=1)

    def process(k_refs, v_refs, base_pos, causal):
        reps = len(k_refs)
        col = lax.broadcasted_iota(jnp.int32, (1, reps * page), 1)
        rel = (col + base_pos).astype(F32)
        s = jnp.concatenate([_dot_nt(q_ref[head_rows(h), :], head_keys(k_refs, h))
                             for h in range(ATTN_HEADS)], axis=0)
        s = s + lane_tile(slope_ref[...], reps) * rel
        if causal:
            t = lax.broadcasted_iota(jnp.int32, s.shape, 0) & (SUBLANES - 1)
            s = jnp.where(lax.broadcasted_iota(jnp.int32, s.shape, 1) <= t, s, NEG_INF)
        m_old = m_ref[...]
        m_new = jnp.maximum(m_old, jnp.max(s, axis=-1, keepdims=True))
        alpha = jnp.exp(m_old - m_new)
        e = jnp.exp(s - lane_tile(m_new, reps))
        l_ref[...] = alpha * l_ref[...] + jnp.sum(e, axis=-1, keepdims=True)
        eb = e.astype(BF16)
        pv = jnp.concatenate([jnp.dot(eb[head_rows(h), :], head_keys(v_refs, h),
                                      preferred_element_type=F32)
                              for h in range(ATTN_HEADS)], axis=0)
        acc_ref[...] = alpha * acc_ref[...] + pv
        m_ref[...] = m_new

    @pl.when(p < n_steps)
    def _():
        process(kc_refs, vc_refs, p * (pps * page) - past_len, False)

    @pl.when(p == n_steps)
    def _():
        process([kn_refs], [vn_refs], 0, True)
        lam = sc_ref[0]
        o = acc_ref[...] / l_ref[...]
        for h in range(ATTN_HEADS):
            r0 = h * ROWS_PER_HEAD
            oh = o[r0:r0 + SUBLANES] - lam * o[r0 + SUBLANES:r0 + ROWS_PER_HEAD]
            o_ref[h * SUBLANES:(h + 1) * SUBLANES, :] = _rms(oh) * g_ref[...] * out_scale


def _decode_attention(page_table, scalars, q_rows, cache_k, cache_v, k_new, v_new, subln_g,
                      page, past_len, out_scale, slopes, pps):
    bsz, n_pages = page_table.shape
    rows = ATTN_HEADS * ROWS_PER_HEAD
    assert cache_k.shape[1:] == (page, HEAD_TILES, SUBLANES, ATTN_DV)
    assert page == ATTN_DV == LANES and n_pages % pps == 0
    page_rows = page * ATTN_HEADS
    n_steps = n_pages // pps
    slope_rows = jnp.broadcast_to(
        jnp.repeat(jnp.asarray(slopes, F32), ROWS_PER_HEAD)[:, None], (rows, page))
    half_block = (None, page, None, SUBLANES, ATTN_DV)

    def cache_spec(slot, tile):
        def index_map(b, p, pt):
            return (pt[b, jnp.minimum(p, n_steps - 1) * pps + slot], 0, tile, 0, 0)
        return pl.BlockSpec(half_block, index_map)

    def new_spec(tile):
        return pl.BlockSpec(half_block, lambda b, p, pt: (b, 0, tile, 0, 0))

    cache_specs = [cache_spec(i, t) for i in range(pps) for t in range(HEAD_TILES)]
    new_specs = [new_spec(t) for t in range(HEAD_TILES)]
    grid_spec = pltpu.PrefetchScalarGridSpec(
        num_scalar_prefetch=1,
        grid=(bsz, n_steps + 1),
        in_specs=[pl.BlockSpec(memory_space=pltpu.SMEM),
                  pl.BlockSpec((None, rows, ATTN_DV), lambda b, p, pt: (b, 0, 0)),
                  pl.BlockSpec((rows, page), lambda b, p, pt: (0, 0))]
                 + cache_specs * 2 + new_specs * 2
                 + [pl.BlockSpec((1, ATTN_DV), lambda b, p, pt: (0, 0))],
        out_specs=pl.BlockSpec((None, ATTN_HEADS * SUBLANES, ATTN_DV), lambda b, p, pt: (b, 0, 0)),
        scratch_shapes=[pltpu.VMEM((rows, page), F32), pltpu.VMEM((rows, page), F32),
                        pltpu.VMEM((rows, ATTN_DV), F32)],
    )
    return pl.pallas_call(
        functools.partial(_decode_attn_kernel, n_steps=n_steps, pps=pps, page=page,
                          past_len=past_len, out_scale=out_scale),
        grid_spec=grid_spec,
        out_shape=jax.ShapeDtypeStruct((bsz, ATTN_HEADS * SUBLANES, ATTN_DV), F32),
        compiler_params=_cparams(("arbitrary", "arbitrary"),
                                 (4 * pps + 4) * page_rows * ATTN_DV * 4 + (12 << 20)),
        name="decode_attn",
    )(page_table, scalars, q_rows, slope_rows,
      *([cache_k] * (pps * HEAD_TILES)), *([cache_v] * (pps * HEAD_TILES)),
      *([k_new] * HEAD_TILES), *([v_new] * HEAD_TILES), subln_g.reshape(1, ATTN_DV))


def _s5_kernel(*refs, tl, chain, emit_y):
    refs = list(refs)
    u_ref, bb_ref, a_ref, h0_ref = refs[:4]
    refs = refs[4:]
    if chain:
        aseg_ref = refs.pop(0)
    if emit_y:
        c_ref, d_ref = refs[:2]
        refs = refs[2:]
        y_ref = refs.pop(0)
    ht_ref = refs.pop(0)
    hs_ref = refs.pop(0)
    if chain:
        init_ref = refs.pop(0)

    n_steps, n_seq, _ = u_ref.shape
    sb = STATE_BLOCK
    ar = a_ref[0:1, :]
    ai = a_ref[1:2, :]

    if chain:
        seg_r = aseg_ref[0:1, :]
        seg_i = aseg_ref[1:2, :]
        for b in range(n_seq // PROMPT_SEGMENTS):
            r = jnp.zeros((1, sb), F32)
            i = jnp.zeros((1, sb), F32)
            for j in range(PROMPT_SEGMENTS):
                row = b * PROMPT_SEGMENTS + j
                init_ref[0, row:row + 1, :] = r
                init_ref[1, row:row + 1, :] = i
                er = h0_ref[0, row:row + 1, :]
                ei = h0_ref[1, row:row + 1, :]
                r, i = seg_r * r - seg_i * i + er, seg_r * i + seg_i * r + ei
        h_init = (init_ref[0], init_ref[1])
    else:
        h_init = (h0_ref[0], h0_ref[1])

    def chunk(c, carry):
        t0 = pl.multiple_of(c * tl, tl)
        x = u_ref[pl.ds(t0, tl)].reshape(tl * n_seq, LANES)
        hs_ref[...] = jnp.dot(x.astype(BF16), bb_ref[...], preferred_element_type=F32)

        def step(t, hc):
            hr, hi = hc
            r0 = pl.multiple_of(t * n_seq, n_seq)
            bur = hs_ref[pl.ds(r0, n_seq), 0:sb]
            bui = hs_ref[pl.ds(r0, n_seq), sb:2 * sb]
            nr = ar * hr - ai * hi + bur
            ni = ar * hi + ai * hr + bui
            if emit_y:
                hs_ref[pl.ds(r0, n_seq), 0:sb] = nr
                hs_ref[pl.ds(r0, n_seq), sb:2 * sb] = ni
            return nr, ni

        carry = lax.fori_loop(0, tl, step, carry, unroll=True)
        if emit_y:
            y = jnp.dot(hs_ref[...].astype(BF16), c_ref[...], preferred_element_type=F32)
            y = jax.nn.gelu(y + d_ref[...] * x)
            y_ref[pl.ds(t0, tl)] = y.reshape(tl, n_seq, LANES)
        return carry

    hr, hi = lax.fori_loop(0, n_steps // tl, chunk, h_init)
    ht_ref[0] = hr
    ht_ref[1] = hi


def _s5_scan(u_tm, bb_blk, a_blk, h0, aseg_blk, c_blk, d_blk, *, tl, chain, emit_y):
    n_steps, n_seq, width = u_tm.shape
    nblk = width // LANES
    sb = STATE_BLOCK
    in_specs = [pl.BlockSpec((n_steps, n_seq, LANES), lambda j: (0, 0, j)),
                pl.BlockSpec((None, LANES, 2 * sb), lambda j: (j, 0, 0)),
                pl.BlockSpec((None, 2, sb), lambda j: (j, 0, 0)),
                pl.BlockSpec((2, n_seq, sb), lambda j: (0, 0, j))]
    args = [u_tm, bb_blk, a_blk, h0]
    if chain:
        in_specs.append(pl.BlockSpec((None, 2, sb), lambda j: (j, 0, 0)))
        args.append(aseg_blk)
    out_specs = []
    out_shape = []
    if emit_y:
        in_specs += [pl.BlockSpec((None, 2 * sb, LANES), lambda j: (j, 0, 0)),
                     pl.BlockSpec((None, 1, LANES), lambda j: (j, 0, 0))]
        args += [c_blk, d_blk]
        out_specs.append(pl.BlockSpec((n_steps, n_seq, LANES), lambda j: (0, 0, j)))
        out_shape.append(jax.ShapeDtypeStruct((n_steps, n_seq, width), F32))
    out_specs.append(pl.BlockSpec((2, n_seq, sb), lambda j: (0, 0, j)))
    out_shape.append(jax.ShapeDtypeStruct((2, n_seq, nblk * sb), F32))
    scratch = [pltpu.VMEM((tl * n_seq, 2 * sb), F32)]
    if chain:
        scratch.append(pltpu.VMEM((2, n_seq, sb), F32))
    io_bytes = 2 * (1 + int(emit_y)) * n_steps * n_seq * LANES * 4
    return pl.pallas_call(
        functools.partial(_s5_kernel, tl=tl, chain=chain, emit_y=emit_y),
        grid=(nblk,),
        in_specs=in_specs, out_specs=out_specs, out_shape=out_shape,
        scratch_shapes=scratch,
        compiler_params=_cparams(("arbitrary",),
                                 io_bytes + 3 * tl * n_seq * 2 * sb * 4 + (12 << 20)),
        name="s5_scan",
    )(*args)


def _glu_kernel(y_ref, w_ref, o_ref):
    y = y_ref[...]
    z = jnp.dot(y.astype(BF16), w_ref[...], preferred_element_type=F32)
    o_ref[...] = (y * jax.nn.sigmoid(z)).astype(o_ref.dtype)


def _glu(y, w, tm):
    m, k = y.shape
    return pl.pallas_call(
        _glu_kernel,
        grid=(m // tm,),
        in_specs=[pl.BlockSpec((tm, k), lambda i: (i, 0)),
                  pl.BlockSpec((k, k), lambda i: (0, 0))],
        out_specs=pl.BlockSpec((tm, k), lambda i: (i, 0)),
        out_shape=jax.ShapeDtypeStruct((m, k), BF16),
        compiler_params=_cparams(("arbitrary",), 2 * k * k * 2 + 6 * tm * k * 4 + (8 << 20)),
        name="glu",
    )(y, w)


def _out_proj_kernel(a1_ref, a2_ref, w1_ref, w2_ref, x_ref, g_ref, o_ref):
    acc = jnp.dot(a1_ref[...], w1_ref[...], preferred_element_type=F32)
    acc = acc + jnp.dot(a2_ref[...], w2_ref[...], preferred_element_type=F32)
    o_ref[...] = x_ref[...] + g_ref[...] * acc


def _out_proj(a1, a2, w, x, gate, tm, tn):
    bsz, t, d = x.shape
    k1, k2 = a1.shape[-1], a2.shape[-1]
    a1 = a1.reshape(bsz, t, k1)
    a2 = a2.reshape(bsz, t, k2)
    if gate.shape[1] == 1:
        g_spec = pl.BlockSpec((None, 1, tn), lambda b, i, j: (b, 0, j))
    else:
        g_spec = pl.BlockSpec((None, tm, tn), lambda b, i, j: (b, i, j))
    return pl.pallas_call(
        _out_proj_kernel,
        grid=(bsz, t // tm, d // tn),
        in_specs=[pl.BlockSpec((None, tm, k1), lambda b, i, j: (b, i, 0)),
                  pl.BlockSpec((None, tm, k2), lambda b, i, j: (b, i, 0)),
                  pl.BlockSpec((k1, tn), lambda b, i, j: (0, j)),
                  pl.BlockSpec((k2, tn), lambda b, i, j: (k1 // k2, j)),
                  pl.BlockSpec((None, tm, tn), lambda b, i, j: (b, i, j)),
                  g_spec],
        out_specs=pl.BlockSpec((None, tm, tn), lambda b, i, j: (b, i, j)),
        out_shape=jax.ShapeDtypeStruct((bsz, t, d), F32),
        compiler_params=_cparams(("arbitrary", "arbitrary", "arbitrary"),
                                 2 * (tm * (k1 + k2) * 2 + (k1 + k2) * tn * 2 + 3 * tm * tn * 4)
                                 + (8 << 20)),
        name="out_proj",
    )(a1, a2, w, w, x, gate)


def _top_rows(s, k):
    n_rows = s.shape[0]
    row = lax.broadcasted_iota(jnp.int32, s.shape, 0).astype(F32)
    k_iota = lax.broadcasted_iota(jnp.int32, (k, s.shape[1]), 0)
    rank = jnp.full(s.shape, NOT_SELECTED, F32)
    val_mat = jnp.zeros((k, s.shape[1]), F32)
    vals, idxs = [], []
    for kk in range(k):
        mx = jnp.max(s, axis=0, keepdims=True)
        idx = jnp.min(jnp.where(s == mx, row, float(n_rows)), axis=0, keepdims=True)
        hit = row == idx
        rank = jnp.where(hit, float(kk), rank)
        s = jnp.where(hit, -jnp.inf, s)
        val_mat = jnp.where(k_iota == kk, mx, val_mat)
        vals.append(mx)
        idxs.append(idx)
    return vals, val_mat, idxs, rank


def _peer_route_kernel(q_ref, keys_ref, p0_ref, th_ref, p1_ref, r1_ref):
    s0 = _dot_nt(keys_ref[0], q_ref[:, :PEER_DHALF])
    s1 = _dot_nt(keys_ref[1], q_ref[:, PEER_DHALF:])
    v0, _, _, rank0 = _top_rows(s0, PEER_TOPK)
    v1, sv1, _, rank1 = _top_rows(s1, PEER_TOPK)
    tm = s0.shape[1]
    n_b = [PEER_TOPK // (a + 1) for a in range(PEER_TOPK)]
    n_cand = sum(n_b)
    pad = (-n_cand) % SUBLANES
    parts = [v0[a] + sv1[:n_b[a]] for a in range(PEER_TOPK)]
    if pad:
        parts.append(jnp.full((pad, tm), -jnp.inf, F32))
    top_s, _, _, cand_rank = _top_rows(jnp.concatenate(parts, axis=0), PEER_TOPK)
    taken = jnp.where(cand_rank < float(PEER_TOPK), 1.0, 0.0)
    z = jnp.zeros((1, tm), F32)
    for kk in range(PEER_TOPK):
        z = z + jnp.exp(top_s[kk] - top_s[0])
    th = jnp.zeros(s0.shape, F32)
    start = 0
    for a in range(PEER_TOPK):
        n_sel = jnp.sum(taken[start:start + n_b[a]], axis=0, keepdims=True)
        th = jnp.where(rank0 == float(a), n_sel, th)
        start += n_b[a]
    p0_ref[...] = jnp.exp(s0 - v0[0])
    th_ref[...] = th
    p1_ref[...] = jnp.exp(s1 - v1[0]) / z
    r1_ref[...] = rank1


def _peer_route(q, keys, tm):
    n = q.shape[0]
    dk = 2 * PEER_DHALF
    spec_out = pl.BlockSpec((None, PEER_NKEYS, tm), lambda i, h: (h, 0, i))
    shape_out = jax.ShapeDtypeStruct((PEER_HEADS, PEER_NKEYS, n), F32)
    return pl.pallas_call(
        _peer_route_kernel,
        grid=(n // tm, PEER_HEADS),
        in_specs=[pl.BlockSpec((tm, dk), lambda i, h: (i, h)),
                  pl.BlockSpec((None, 2, PEER_NKEYS, PEER_DHALF), lambda i, h: (h, 0, 0, 0))],
        out_specs=[spec_out] * 4,
        out_shape=[shape_out] * 4,
        compiler_params=_cparams(("arbitrary", "arbitrary"),
                                 4 * tm * dk * 2 + 48 * PEER_NKEYS * tm * 4 + (8 << 20)),
        name="peer_route",
    )(q, keys)


def _peer_dense_kernel(h_ref, u_ref, vt_ref, p0_ref, th_ref, p1_ref, r1_ref, o_ref, w_ref, *, te):
    e = pl.program_id(1)

    @pl.when(e == 0)
    def _():
        o_ref[...] = jnp.zeros(o_ref.shape, F32)

    for ib in range(te // PEER_NKEYS):
        i0 = e * (te // PEER_NKEYS) + ib
        th_rows = [th_ref[h, pl.ds(i0, 1), :] for h in range(PEER_HEADS)]
        p0_rows = [p0_ref[h, pl.ds(i0, 1), :] for h in range(PEER_HEADS)]
        for c in range(w_ref.shape[1] // LANES):
            tok = slice(c * LANES, (c + 1) * LANES)
            w = jnp.zeros((PEER_NKEYS, LANES), F32)
            for h in range(PEER_HEADS):
                w = w + jnp.where(r1_ref[h, :, tok] < th_rows[h][:, tok],
                                  p1_ref[h, :, tok] * p0_rows[h][:, tok], 0.0)
            w_ref[ib * PEER_NKEYS:(ib + 1) * PEER_NKEYS, tok] = w

    act = jax.nn.gelu(_dot_nt(u_ref[...], h_ref[...]))
    coef = (w_ref[...] * act).astype(BF16)
    o_ref[...] += jnp.dot(vt_ref[...], coef, preferred_element_type=F32)


def _peer_dense(h2, u, v_t, sel, tm, te):
    n, d = h2.shape
    n_exp = u.shape[0]
    once = pl.Buffered(1)
    sel_spec = pl.BlockSpec((PEER_HEADS, PEER_NKEYS, tm), lambda i, e: (0, 0, i),
                            pipeline_mode=once)
    return pl.pallas_call(
        functools.partial(_peer_dense_kernel, te=te),
        grid=(n // tm, n_exp // te),
        in_specs=[pl.BlockSpec((tm, d), lambda i, e: (i, 0), pipeline_mode=once),
                  pl.BlockSpec((te, d), lambda i, e: (e, 0)),
                  pl.BlockSpec((d, te), lambda i, e: (0, e))] + [sel_spec] * 4,
        out_specs=pl.BlockSpec((d, tm), lambda i, e: (0, i), pipeline_mode=once),
        out_shape=jax.ShapeDtypeStruct((d, n), F32),
        scratch_shapes=[pltpu.VMEM((te, tm), F32)],
        compiler_params=_cparams(
            ("arbitrary", "arbitrary"),
            tm * d * 2 + 2 * 2 * te * d * 2 + 4 * PEER_HEADS * PEER_NKEYS * tm * 4 + d * tm * 4
            + 8 * te * tm * 4 + (6 << 20)),
        name="peer_dense",
    )(h2, u, v_t, *sel)


def _peer_out_kernel(ot_ref, x_ref, g_ref, fg_ref, y_ref):
    x2 = x_ref[...] + g_ref[...] * ot_ref[...].T
    y_ref[...] = _rms(x2) * fg_ref[...]


def _peer_out(o_t, x, gate, final_g, tm):
    bsz, t, d = x.shape
    nt = t // tm
    return pl.pallas_call(
        _peer_out_kernel,
        grid=(bsz, nt),
        in_specs=[pl.BlockSpec((d, tm), lambda b, i: (0, b * nt + i)),
                  pl.BlockSpec((None, tm, d), lambda b, i: (b, i, 0)),
                  _mod_spec(gate, tm, d),
                  pl.BlockSpec((1, d), lambda b, i: (0, 0))],
        out_specs=pl.BlockSpec((None, tm, d), lambda b, i: (b, i, 0)),
        out_shape=jax.ShapeDtypeStruct((bsz, t, d), F32),
        compiler_params=_cparams(("arbitrary", "arbitrary"), 10 * tm * d * 4 + (8 << 20)),
        name="peer_out",
    )(o_t, x, gate, final_g.reshape(1, d))


def _alibi_slopes(n):
    return tuple(float(np.float32(2.0 ** (-8.0 * (h + 1) / n))) for h in range(n))


def _s5_params(lam_re, lam_im, log_dt, b_re, b_im, c_re, c_im, d_skip, seg_len):
    dt = jnp.exp(log_dt.astype(F32))[:, None]
    lr, li = lam_re.astype(F32), lam_im.astype(F32)
    mag = jnp.exp(lr * dt)
    a_re, a_im = mag * jnp.cos(li * dt), mag * jnp.sin(li * dt)
    den = lr * lr + li * li
    nr, ni = a_re - 1.0, a_im
    f_re, f_im = (nr * lr + ni * li) / den, (ni * lr - nr * li) / den
    br, bi = b_re.astype(F32), b_im.astype(F32)
    bb_re = f_re[..., None] * br - f_im[..., None] * bi
    bb_im = f_re[..., None] * bi + f_im[..., None] * br
    mag_s = jnp.exp(lr * dt * seg_len)
    s_re, s_im = mag_s * jnp.cos(li * dt * seg_len), mag_s * jnp.sin(li * dt * seg_len)

    g = lam_re.shape[0]
    nblk = g // GROUPS_PER_BLOCK
    eye = jnp.eye(GROUPS_PER_BLOCK, dtype=F32)

    def in_blocks(w):
        w = w.reshape(nblk, GROUPS_PER_BLOCK, SSM_STATE, SSM_GROUP)
        return jnp.einsum('jgnc,gk->jgckn', w, eye).reshape(nblk, LANES, STATE_BLOCK)

    def out_blocks(w):
        w = w.reshape(nblk, GROUPS_PER_BLOCK, SSM_GROUP, SSM_STATE)
        return jnp.einsum('jgcn,gk->jgnkc', w, eye).reshape(nblk, STATE_BLOCK, LANES)

    bb_blk = jnp.concatenate([in_blocks(bb_re), in_blocks(bb_im)], axis=2).astype(BF16)
    c_blk = jnp.concatenate([out_blocks(c_re.astype(F32)), -out_blocks(c_im.astype(F32))],
                            axis=1).astype(BF16)
    a_blk = jnp.stack([a_re.reshape(nblk, STATE_BLOCK), a_im.reshape(nblk, STATE_BLOCK)], axis=1)
    aseg_blk = jnp.stack([s_re.reshape(nblk, STATE_BLOCK), s_im.reshape(nblk, STATE_BLOCK)], axis=1)
    d_blk = d_skip.astype(F32).reshape(nblk, 1, LANES)
    return bb_blk, a_blk, aseg_blk, c_blk, d_blk


def _pick(n, pref):
    t = min(n, pref)
    while n % t:
        t //= 2
    return t


def kernel(x_prompt, x_sample, cache_k, cache_v, page_table, state_ssm_re, state_ssm_im,
           c_prompt, c_sample, w_ada, b_ada, norm1_g, norm2_g, w_in,
           lam_q1, lam_k1, lam_q2, lam_k2, subln_g,
           ssm_lam_re, ssm_lam_im, ssm_log_dt, ssm_b_re, ssm_b_im, ssm_c_re, ssm_c_im, ssm_d,
           w_glu, w_out, peer_wq, peer_sub_keys, peer_u, peer_v, final_g):
    depth = w_ada.shape[0]
    assert depth == 1, "kernel() fuses the final norm into the layer and supports DEPTH == 1"
    bsz, seq, d_model = x_prompt.shape
    dec_b, dec_t, _ = x_sample.shape
    n_p, n_s = bsz * seq, dec_b * dec_t
    n_pages, page = page_table.shape[1], cache_k.shape[2]
    past_len = n_pages * page
    attn_w = ATTN_HEADS * ATTN_DV
    ssm_w = w_in.shape[2] - 3 * attn_w
    n_groups = ssm_w // SSM_GROUP
    slopes = _alibi_slopes(ATTN_HEADS)
    seg_len = seq // PROMPT_SEGMENTS
    n_seq_p = bsz * PROMPT_SEGMENTS

    xp = x_prompt
    xs = x_sample.reshape(1, n_s, d_model)
    outs = [[] for _ in range(8)]

    for l in range(depth):
        lam_init = 0.8 - 0.6 * math.exp(-0.3 * l)
        out_scale = 1.0 - lam_init
        lam = (jnp.exp(jnp.sum(lam_q1[l].astype(F32) * lam_k1[l].astype(F32)))
               - jnp.exp(jnp.sum(lam_q2[l].astype(F32) * lam_k2[l].astype(F32))) + lam_init)
        scalars = jnp.concatenate([lam.reshape(1), jnp.asarray(slopes, F32)])

        c_all = jnp.concatenate([c_prompt, c_sample], axis=0)
        pad = (-c_all.shape[0]) % SUBLANES
        c_all = jnp.pad(c_all, ((0, pad), (0, 0)))
        mod = _adaln(c_all, w_ada[l], b_ada[l])
        mod_p = [m[:, None, :] for m in jnp.split(mod[:bsz], N_MOD, axis=-1)]
        mod_s = [jnp.repeat(m, dec_t, axis=0)[None] for m in
                 jnp.split(mod[bsz:bsz + dec_b], N_MOD, axis=-1)]

        w_in_b = w_in[l].astype(BF16)
        w_glu_b = w_glu[l].astype(BF16)
        w_out_b = w_out[l].astype(BF16)
        wq_b = peer_wq[l].astype(BF16)
        keys_b = peer_sub_keys[l].astype(BF16)
        u_b = peer_u[l].astype(BF16)
        v_t = peer_v[l].T.astype(BF16)
        bb_blk, a_blk, aseg_blk, c_blk, d_blk = _s5_params(
            ssm_lam_re[l], ssm_lam_im[l], ssm_log_dt[l], ssm_b_re[l], ssm_b_im[l],
            ssm_c_re[l], ssm_c_im[l], ssm_d[l], seg_len)

        def front(x, mods, tm):
            sh1, sc1 = mods[0], mods[1]
            h = _norm_mod(x, norm1_g[l], sh1, sc1, tm).reshape(-1, d_model)
            tmm = _pick(h.shape[0], 1024)
            (q_b, q_f) = _matmul_cols(h, w_in_b, 0, attn_w, (BF16, F32), tmm, 512)
            (k_f, k_b) = _matmul_cols(h, w_in_b, attn_w, attn_w, (F32, BF16), tmm, 512)
            (v_f, v_b) = _matmul_cols(h, w_in_b, 2 * attn_w, attn_w, (F32, BF16), tmm, 512)
            (u_f,) = _matmul_cols(h, w_in_b, 3 * attn_w, ssm_w, (F32,), tmm, 512)
            return q_b, q_f, k_f, k_b, v_f, v_b, u_f

        def back(x, mods, o_attn, o_ssm, tm):
            g1, sh2, sc2, g2 = mods[2], mods[3], mods[4], mods[5]
            n = x.shape[0] * x.shape[1]
            x1 = _out_proj(o_attn, o_ssm, w_out_b, x, g1, _pick(x.shape[1], 1024), 512)
            h2 = _norm_mod(x1, norm2_g[l], sh2, sc2, tm).reshape(n, d_model)
            tmp = _pick(n, 512)
            (q_peer,) = _matmul_cols(h2, wq_b, 0, wq_b.shape[1], (BF16,), _pick(n, 1024), 512)
            sel = _peer_route(q_peer, keys_b, tmp)
            o_t = _peer_dense(h2, u_b, v_t, sel, tmp, 512)
            return _peer_out(o_t, x1, g2, final_g, _pick(x.shape[1], 256))

        q_b, _, k_f, k_b, v_f, v_b, u_f = front(xp, mod_p, 256)
        o_attn_p = _prompt_attention(scalars, q_b, k_b, v_b, subln_g[l], bsz, seq, out_scale,
                                     256, 8)
        u_tm = u_f.reshape(bsz, PROMPT_SEGMENTS, seg_len, ssm_w).transpose(2, 0, 1, 3)
        u_tm = u_tm.reshape(seg_len, n_seq_p, ssm_w)
        zeros_p = jnp.zeros((2, n_seq_p, n_groups * SSM_STATE), F32)
        (seg_end,) = _s5_scan(u_tm, bb_blk, a_blk, zeros_p, None, None, None,
                              tl=16, chain=False, emit_y=False)
        y_tm, h_end = _s5_scan(u_tm, bb_blk, a_blk, seg_end, aseg_blk, c_blk, d_blk,
                               tl=16, chain=True, emit_y=True)
        y_p = y_tm.reshape(seg_len, bsz, PROMPT_SEGMENTS, ssm_w).transpose(1, 2, 0, 3)
        o_ssm_p = _glu(y_p.reshape(n_p, ssm_w), w_glu_b, 256)
        y_prompt_l = back(xp, mod_p, o_attn_p, o_ssm_p, 256)
        h_end = h_end.reshape(2, bsz, PROMPT_SEGMENTS, n_groups, SSM_STATE)[:, :, -1]
        outs[0].append(k_f.reshape(bsz, seq, ATTN_HEADS, ATTN_DV))
        outs[1].append(v_f.reshape(bsz, seq, ATTN_HEADS, ATTN_DV))
        outs[4].append(h_end[0])
        outs[5].append(h_end[1])

        _, q_f, k_f, _, v_f, _, u_f = front(xs, mod_s, n_s)
        qh = q_f.reshape(dec_b, dec_t, ATTN_HEADS, ATTN_DV).transpose(0, 2, 1, 3)
        lane = jnp.arange(ATTN_DV) < ATTN_DH
        qh = jnp.stack([jnp.where(lane, qh, 0.0), jnp.where(lane, 0.0, qh)], axis=2)
        qh = jnp.pad(qh, ((0, 0), (0, 0), (0, 0), (0, SUBLANES - dec_t), (0, 0)))
        q_rows = (qh * (ATTN_DH ** -0.5)).astype(BF16).reshape(
            dec_b, ATTN_HEADS * ROWS_PER_HEAD, ATTN_DV)
        kv_view = (-1, page, HEAD_TILES, SUBLANES, ATTN_DV)
        new_pad = ((0, 0), (0, page - dec_t), (0, 0), (0, 0), (0, 0))
        k_new = jnp.pad(k_f.reshape((dec_b, dec_t) + kv_view[2:]), new_pad)
        v_new = jnp.pad(v_f.reshape((dec_b, dec_t) + kv_view[2:]), new_pad)
        o_rows = _decode_attention(
            page_table, scalars, q_rows, cache_k.reshape(kv_view), cache_v.reshape(kv_view),
            k_new, v_new, subln_g[l], page, past_len, out_scale, slopes, 8)
        o_attn_s = o_rows.reshape(dec_b, ATTN_HEADS, SUBLANES, ATTN_DV)[:, :, :dec_t]
        o_attn_s = o_attn_s.transpose(0, 2, 1, 3).reshape(n_s, attn_w).astype(BF16)
        u_tm = u_f.reshape(dec_b, dec_t, ssm_w).transpose(1, 0, 2)
        h0 = jnp.stack([state_ssm_re[l].reshape(dec_b, -1), state_ssm_im[l].reshape(dec_b, -1)])
        y_tm, h_end = _s5_scan(u_tm, bb_blk, a_blk, h0.astype(F32), None, c_blk, d_blk,
                               tl=dec_t, chain=False, emit_y=True)
        o_ssm_s = _glu(y_tm.transpose(1, 0, 2).reshape(n_s, ssm_w), w_glu_b, n_s)
        y_sample_l = back(xs, mod_s, o_attn_s, o_ssm_s, n_s)
        outs[2].append(k_f.reshape(dec_b, dec_t, ATTN_HEADS, ATTN_DV))
        outs[3].append(v_f.reshape(dec_b, dec_t, ATTN_HEADS, ATTN_DV))
        outs[6].append(h_end[0].reshape(dec_b, n_groups, SSM_STATE))
        outs[7].append(h_end[1].reshape(dec_b, n_groups, SSM_STATE))

    y_prompt = y_prompt_l
    y_sample = y_sample_l.reshape(dec_b, dec_t, d_model)
    return (y_prompt, y_sample) + tuple(jnp.stack(o) for o in outs)
```

```python
import functools
import math

import numpy as np
import jax
import jax.numpy as jnp
from jax import lax
from jax.experimental import pallas as pl
from jax.experimental.pallas import tpu as pltpu

F32 = jnp.float32
BF16 = jnp.bfloat16

V7X_VMEM_BYTES = 64 * 1024 * 1024
LANES = 128
SUBLANES = 8

EPS = 1e-6
NEG_INF = -1e30

ATTN_HEADS = 16
ATTN_DH = 64
ATTN_DV = 2 * ATTN_DH
SSM_GROUP = 16
SSM_STATE = 64
GROUPS_PER_BLOCK = LANES // SSM_GROUP
STATE_BLOCK = GROUPS_PER_BLOCK * SSM_STATE
PROMPT_SEGMENTS = 8
PEER_HEADS = 8
PEER_NKEYS = 128
PEER_DHALF = 128
PEER_TOPK = 16
N_MOD = 6
NOT_SELECTED = 99.0


def _cparams(semantics, vmem_bytes):
    return pltpu.CompilerParams(dimension_semantics=semantics,
                                vmem_limit_bytes=min(int(vmem_bytes), V7X_VMEM_BYTES - (4 << 20)))


def _dot_nt(a, b):
    return lax.dot_general(a, b, (((1,), (1,)), ((), ())), preferred_element_type=F32)


def _rms(x):
    return x * lax.rsqrt(jnp.mean(x * x, axis=-1, keepdims=True) + EPS)


def _adaln_kernel(c_ref, w_ref, b_ref, o_ref):
    a = jax.nn.silu(c_ref[...])
    o_ref[...] = jnp.dot(a.astype(BF16), w_ref[...].astype(BF16),
                         preferred_element_type=F32) + b_ref[...]


def _adaln(c, w, b, tn=1024):
    m, k = c.shape
    n = w.shape[1]
    return pl.pallas_call(
        _adaln_kernel,
        grid=(n // tn,),
        in_specs=[pl.BlockSpec((m, k), lambda j: (0, 0)),
                  pl.BlockSpec((k, tn), lambda j: (0, j)),
                  pl.BlockSpec((1, tn), lambda j: (0, j))],
        out_specs=pl.BlockSpec((m, tn), lambda j: (0, j)),
        out_shape=jax.ShapeDtypeStruct((m, n), F32),
        compiler_params=_cparams(("arbitrary",), 2 * k * tn * 4 + (8 << 20)),
        name="adaln",
    )(c, w, b.reshape(1, n))


def _norm_mod_kernel(x_ref, g_ref, sh_ref, sc_ref, o_ref):
    y = _rms(x_ref[...]) * g_ref[...]
    o_ref[...] = (y * (1.0 + sc_ref[...]) + sh_ref[...]).astype(o_ref.dtype)


def _mod_spec(mod, tm, d):
    if mod.shape[1] == 1:
        return pl.BlockSpec((None, 1, d), lambda b, i: (b, 0, 0))
    return pl.BlockSpec((None, tm, d), lambda b, i: (b, i, 0))


def _norm_mod(x, g, shift, scale, tm):
    bsz, t, d = x.shape
    return pl.pallas_call(
        _norm_mod_kernel,
        grid=(bsz, t // tm),
        in_specs=[pl.BlockSpec((None, tm, d), lambda b, i: (b, i, 0)),
                  pl.BlockSpec((1, d), lambda b, i: (0, 0)),
                  _mod_spec(shift, tm, d), _mod_spec(scale, tm, d)],
        out_specs=pl.BlockSpec((None, tm, d), lambda b, i: (b, i, 0)),
        out_shape=jax.ShapeDtypeStruct((bsz, t, d), BF16),
        compiler_params=_cparams(("arbitrary", "arbitrary"), 8 * tm * d * 4 + (8 << 20)),
        name="norm_mod",
    )(x, g.reshape(1, d), shift, scale)


def _mm_kernel(a_ref, b_ref, *o_refs):
    acc = jnp.dot(a_ref[...], b_ref[...], preferred_element_type=F32)
    for o_ref in o_refs:
        o_ref[...] = acc.astype(o_ref.dtype)


def _matmul_cols(a, b, col0, ncols, out_dtypes, tm, tn):
    m, k = a.shape
    joff = col0 // tn
    out_bytes = sum(jnp.dtype(dt).itemsize for dt in out_dtypes)
    return pl.pallas_call(
        _mm_kernel,
        grid=(m // tm, ncols // tn),
        in_specs=[pl.BlockSpec((tm, k), lambda i, j: (i, 0), pipeline_mode=pl.Buffered(1)),
                  pl.BlockSpec((k, tn), lambda i, j: (0, j + joff))],
        out_specs=[pl.BlockSpec((tm, tn), lambda i, j: (i, j)) for _ in out_dtypes],
        out_shape=[jax.ShapeDtypeStruct((m, ncols), dt) for dt in out_dtypes],
        compiler_params=_cparams(("arbitrary", "arbitrary"),
                                 tm * k * 2 + 2 * (k * tn * 2 + tm * tn * out_bytes)
                                 + 2 * tm * tn * 4 + (8 << 20)),
        name="matmul_cols",
    )(a, b)


def _prompt_attn_kernel(sc_ref, q_ref, k_ref, v_ref, g_ref, o_ref, m_ref, l_ref, acc_ref,
                        *, tq, hp, out_scale):
    hg = pl.program_id(1)
    qi = pl.program_id(2)
    lam = sc_ref[0]
    slopes = [sc_ref[1 + hg * hp + j] for j in range(hp)]

    def head_cols(j):
        return slice(j * ATTN_DV, (j + 1) * ATTN_DV)

    qqs = []
    for j in range(hp):
        q = q_ref[:, head_cols(j)] * jnp.asarray(ATTN_DH ** -0.5, BF16)
        lane = lax.broadcasted_iota(jnp.int32, q.shape, 1)
        zero = jnp.zeros_like(q)
        qqs.append(jnp.concatenate([jnp.where(lane < ATTN_DH, q, zero),
                                    jnp.where(lane >= ATTN_DH, q, zero)], axis=0))

    m_ref[...] = jnp.full(m_ref.shape, NEG_INF, F32)
    l_ref[...] = jnp.zeros(l_ref.shape, F32)
    acc_ref[...] = jnp.zeros(acc_ref.shape, F32)
    col = lax.broadcasted_iota(jnp.int32, (1, tq), 1)
    lane_reps = tq // LANES

    def block(kj, masked):
        off = pl.multiple_of(kj * tq, tq)
        rel = (col + (kj - qi) * tq).astype(F32)
        for j in range(hp):
            k = k_ref[pl.ds(off, tq), head_cols(j)]
            v = v_ref[pl.ds(off, tq), head_cols(j)]
            s = _dot_nt(qqs[j], k) + slopes[j] * rel
            if masked:
                row = lax.broadcasted_iota(jnp.int32, s.shape, 0)
                row = jnp.where(row >= tq, row - tq, row)
                s = jnp.where(row >= lax.broadcasted_iota(jnp.int32, s.shape, 1), s, NEG_INF)
            m_old = m_ref[j]
            m_new = jnp.maximum(m_old, jnp.max(s, axis=-1, keepdims=True))
            alpha = jnp.exp(m_old - m_new)
            p = jnp.exp(s - jnp.concatenate([m_new] * lane_reps, axis=1))
            l_ref[j] = alpha * l_ref[j] + jnp.sum(p, axis=-1, keepdims=True)
            acc_ref[j] = alpha * acc_ref[j] + jnp.dot(p.astype(BF16), v,
                                                      preferred_element_type=F32)
            m_ref[j] = m_new

    def full_block(kj, carry):
        block(kj, False)
        return carry

    lax.fori_loop(0, qi, full_block, 0)
    block(qi, True)

    for j in range(hp):
        o = acc_ref[j] / l_ref[j]
        o = o[:tq] - lam * o[tq:]
        o_ref[:, head_cols(j)] = (_rms(o) * g_ref[...] * out_scale).astype(o_ref.dtype)


def _prompt_attention(scalars, q, k, v, subln_g, bsz, t, out_scale, tq, hp):
    n, width = q.shape
    nq = t // tq
    assert ATTN_DV == LANES and tq % LANES == 0 and ATTN_HEADS % hp == 0
    wb = hp * ATTN_DV
    return pl.pallas_call(
        functools.partial(_prompt_attn_kernel, tq=tq, hp=hp, out_scale=out_scale),
        grid=(bsz, ATTN_HEADS // hp, nq),
        in_specs=[pl.BlockSpec(memory_space=pltpu.SMEM),
                  pl.BlockSpec((tq, wb), lambda b, h, i: (b * nq + i, h)),
                  pl.BlockSpec((t, wb), lambda b, h, i: (b, h)),
                  pl.BlockSpec((t, wb), lambda b, h, i: (b, h)),
                  pl.BlockSpec((1, ATTN_DV), lambda b, h, i: (0, 0))],
        out_specs=pl.BlockSpec((tq, wb), lambda b, h, i: (b * nq + i, h)),
        out_shape=jax.ShapeDtypeStruct((n, width), BF16),
        scratch_shapes=[pltpu.VMEM((hp, 2 * tq, LANES), F32), pltpu.VMEM((hp, 2 * tq, LANES), F32),
                        pltpu.VMEM((hp, 2 * tq, ATTN_DV), F32)],
        compiler_params=_cparams(("arbitrary", "arbitrary", "arbitrary"),
                                 8 * t * wb * 2 + 16 * hp * tq * tq * 4 + (8 << 20)),
        name="prompt_attn",
    )(scalars, q, k, v, subln_g.reshape(1, ATTN_DV))


ROWS_PER_HEAD = 2 * SUBLANES
HEAD_TILES = ATTN_HEADS // SUBLANES


def _decode_attn_kernel(pt_ref, sc_ref, q_ref, slope_ref, *refs, n_steps, pps, page, past_len,
                        out_scale):
    del pt_ref
    nt = HEAD_TILES
    kc_refs = [refs[i * nt:(i + 1) * nt] for i in range(pps)]
    vc_refs = [refs[(pps + i) * nt:(pps + i + 1) * nt] for i in range(pps)]
    rest = refs[2 * pps * nt:]
    kn_refs, vn_refs = rest[:nt], rest[nt:2 * nt]
    g_ref, o_ref, m_ref, l_ref, acc_ref = rest[2 * nt:]
    p = pl.program_id(1)

    @pl.when(p == 0)
    def _():
        m_ref[...] = jnp.full(m_ref.shape, NEG_INF, F32)
        l_ref[...] = jnp.zeros(l_ref.shape, F32)
        acc_ref[...] = jnp.zeros(acc_ref.shape, F32)

    def head_rows(h):
        return slice(h * ROWS_PER_HEAD, (h + 1) * ROWS_PER_HEAD)

    def head_keys(page_refs, h):
        parts = [tiles[h // SUBLANES].reshape(page * SUBLANES, ATTN_DV)[
            pl.ds(h % SUBLANES, page, stride=SUBLANES), :].astype(BF16) for tiles in page_refs]
        return parts[0] if len(parts) == 1 else jnp.concatenate(parts, axis=0)

    def lane_tile(x, reps):
        return x if reps == 1 else jnp.concatenate([x] * reps, axis=1)

    def process(k_refs, v_refs, base_pos, causal):
        reps = len(k_refs)
        col = lax.broadcasted_iota(jnp.int32, (1, reps * page), 1)
        rel = (col + base_pos).astype(F32)
        s = jnp.concatenate([_dot_nt(q_ref[head_rows(h), :], head_keys(k_refs, h))
                             for h in range(ATTN_HEADS)], axis=0)
        s = s + lane_tile(slope_ref[...], reps) * rel
        if causal:
            t = lax.broadcasted_iota(jnp.int32, s.shape, 0) & (SUBLANES - 1)
            s = jnp.where(lax.broadcasted_iota(jnp.int32, s.shape, 1) <= t, s, NEG_INF)
        m_old = m_ref[...]
        m_new = jnp.maximum(m_old, jnp.max(s, axis=-1, keepdims=True))
        alpha = jnp.exp(m_old - m_new)
        e = jnp.exp(s - lane_tile(m_new, reps))
        l_ref[...] = alpha * l_ref[...] + jnp.sum(e, axis=-1, keepdims=True)
        eb = e.astype(BF16)
        pv = jnp.concatenate([jnp.dot(eb[head_rows(h), :], head_keys(v_refs, h),
                                      preferred_element_type=F32)
                              for h in range(ATTN_HEADS)], axis=0)
        acc_ref[...] = alpha * acc_ref[...] + pv
        m_ref[...] = m_new

    @pl.when(p < n_steps)
    def _():
        process(kc_refs, vc_refs, p * (pps * page) - past_len, False)

    @pl.when(p == n_steps)
    def _():
        process([kn_refs], [vn_refs], 0, True)
        lam = sc_ref[0]
        o = acc_ref[...] / l_ref[...]
        for h in range(ATTN_HEADS):
            r0 = h * ROWS_PER_HEAD
            oh = o[r0:r0 + SUBLANES] - lam * o[r0 + SUBLANES:r0 + ROWS_PER_HEAD]
            o_ref[h * SUBLANES:(h + 1) * SUBLANES, :] = _rms(oh) * g_ref[...] * out_scale


def _decode_attention(page_table, scalars, q_rows, cache_k, cache_v, k_new, v_new, subln_g,
                      page, past_len, out_scale, slopes, pps):
    bsz, n_pages = page_table.shape
    rows = ATTN_HEADS * ROWS_PER_HEAD
    assert cache_k.shape[1:] == (page, HEAD_TILES, SUBLANES, ATTN_DV)
    assert page == ATTN_DV == LANES and n_pages % pps == 0
    page_rows = page * ATTN_HEADS
    n_steps = n_pages // pps
    slope_rows = jnp.broadcast_to(
        jnp.repeat(jnp.asarray(slopes, F32), ROWS_PER_HEAD)[:, None], (rows, page))
    half_block = (None, page, None, SUBLANES, ATTN_DV)

    def cache_spec(slot, tile):
        def index_map(b, p, pt):
            return (pt[b, jnp.minimum(p, n_steps - 1) * pps + slot], 0, tile, 0, 0)
        return pl.BlockSpec(half_block, index_map)

    def new_spec(tile):
        return pl.BlockSpec(half_block, lambda b, p, pt: (b, 0, tile, 0, 0))

    cache_specs = [cache_spec(i, t) for i in range(pps) for t in range(HEAD_TILES)]
    new_specs = [new_spec(t) for t in range(HEAD_TILES)]
    grid_spec = pltpu.PrefetchScalarGridSpec(
        num_scalar_prefetch=1,
        grid=(bsz, n_steps + 1),
        in_specs=[pl.BlockSpec(memory_space=pltpu.SMEM),
                  pl.BlockSpec((None, rows, ATTN_DV), lambda b, p, pt: (b, 0, 0)),
                  pl.BlockSpec((rows, page), lambda b, p, pt: (0, 0))]
                 + cache_specs * 2 + new_specs * 2
                 + [pl.BlockSpec((1, ATTN_DV), lambda b, p, pt: (0, 0))],
        out_specs=pl.BlockSpec((None, ATTN_HEADS * SUBLANES, ATTN_DV), lambda b, p, pt: (b, 0, 0)),
        scratch_shapes=[pltpu.VMEM((rows, page), F32), pltpu.VMEM((rows, page), F32),
                        pltpu.VMEM((rows, ATTN_DV), F32)],
    )
    return pl.pallas_call(
        functools.partial(_decode_attn_kernel, n_steps=n_steps, pps=pps, page=page,
                          past_len=past_len, out_scale=out_scale),
        grid_spec=grid_spec,
        out_shape=jax.ShapeDtypeStruct((bsz, ATTN_HEADS * SUBLANES, ATTN_DV), F32),
        compiler_params=_cparams(("arbitrary", "arbitrary"),
                                 (4 * pps + 4) * page_rows * ATTN_DV * 4 + (12 << 20)),
        name="decode_attn",
    )(page_table, scalars, q_rows, slope_rows,
      *([cache_k] * (pps * HEAD_TILES)), *([cache_v] * (pps * HEAD_TILES)),
      *([k_new] * HEAD_TILES), *([v_new] * HEAD_TILES), subln_g.reshape(1, ATTN_DV))


def _s5_kernel(*refs, tl, chain, emit_y):
    refs = list(refs)
    u_ref, bb_ref, a_ref, h0_ref = refs[:4]
    refs = refs[4:]
    if chain:
        aseg_ref = refs.pop(0)
    if emit_y:
        c_ref, d_ref = refs[:2]
        refs = refs[2:]
        y_ref = refs.pop(0)
    ht_ref = refs.pop(0)
    hs_ref = refs.pop(0)
    if chain:
        init_ref = refs.pop(0)

    n_steps, n_seq, _ = u_ref.shape
    sb = STATE_BLOCK
    ar = a_ref[0:1, :]
    ai = a_ref[1:2, :]

    if chain:
        seg_r = aseg_ref[0:1, :]
        seg_i = aseg_ref[1:2, :]
        for b in range(n_seq // PROMPT_SEGMENTS):
            r = jnp.zeros((1, sb), F32)
            i = jnp.zeros((1, sb), F32)
            for j in range(PROMPT_SEGMENTS):
                row = b * PROMPT_SEGMENTS + j
                init_ref[0, row:row + 1, :] = r
                init_ref[1, row:row + 1, :] = i
                er = h0_ref[0, row:row + 1, :]
                ei = h0_ref[1, row:row + 1, :]
                r, i = seg_r * r - seg_i * i + er, seg_r * i + seg_i * r + ei
        h_init = (init_ref[0], init_ref[1])
    else:
        h_init = (h0_ref[0], h0_ref[1])

    def chunk(c, carry):
        t0 = pl.multiple_of(c * tl, tl)
        x = u_ref[pl.ds(t0, tl)].reshape(tl * n_seq, LANES)
        hs_ref[...] = jnp.dot(x.astype(BF16), bb_ref[...], preferred_element_type=F32)

        def step(t, hc):
            hr, hi = hc
            r0 = pl.multiple_of(t * n_seq, n_seq)
            bur = hs_ref[pl.ds(r0, n_seq), 0:sb]
            bui = hs_ref[pl.ds(r0, n_seq), sb:2 * sb]
            nr = ar * hr - ai * hi + bur
            ni = ar * hi + ai * hr + bui
            if emit_y:
                hs_ref[pl.ds(r0, n_seq), 0:sb] = nr
                hs_ref[pl.ds(r0, n_seq), sb:2 * sb] = ni
            return nr, ni

        carry = lax.fori_loop(0, tl, step, carry, unroll=True)
        if emit_y:
            y = jnp.dot(hs_ref[...].astype(BF16), c_ref[...], preferred_element_type=F32)
            y = jax.nn.gelu(y + d_ref[...] * x)
            y_ref[pl.ds(t0, tl)] = y.reshape(tl, n_seq, LANES)
        return carry

    hr, hi = lax.fori_loop(0, n_steps // tl, chunk, h_init)
    ht_ref[0] = hr
    ht_ref[1] = hi


def _s5_scan(u_tm, bb_blk, a_blk, h0, aseg_blk, c_blk, d_blk, *, tl, chain, emit_y):
    n_steps, n_seq, width = u_tm.shape
    nblk = width // LANES
    sb = STATE_BLOCK
    in_specs = [pl.BlockSpec((n_steps, n_seq, LANES), lambda j: (0, 0, j)),
                pl.BlockSpec((None, LANES, 2 * sb), lambda j: (j, 0, 0)),
                pl.BlockSpec((None, 2, sb), lambda j: (j, 0, 0)),
                pl.BlockSpec((2, n_seq, sb), lambda j: (0, 0, j))]
    args = [u_tm, bb_blk, a_blk, h0]
    if chain:
        in_specs.append(pl.BlockSpec((None, 2, sb), lambda j: (j, 0, 0)))
        args.append(aseg_blk)
    out_specs = []
    out_shape = []
    if emit_y:
        in_specs += [pl.BlockSpec((None, 2 * sb, LANES), lambda j: (j, 0, 0)),
                     pl.BlockSpec((None, 1, LANES), lambda j: (j, 0, 0))]
        args += [c_blk, d_blk]
        out_specs.append(pl.BlockSpec((n_steps, n_seq, LANES), lambda j: (0, 0, j)))
        out_shape.append(jax.ShapeDtypeStruct((n_steps, n_seq, width), F32))
    out_specs.append(pl.BlockSpec((2, n_seq, sb), lambda j: (0, 0, j)))
    out_shape.append(jax.ShapeDtypeStruct((2, n_seq, nblk * sb), F32))
    scratch = [pltpu.VMEM((tl * n_seq, 2 * sb), F32)]
    if chain:
        scratch.append(pltpu.VMEM((2, n_seq, sb), F32))
    io_bytes = 2 * (1 + int(emit_y)) * n_steps * n_seq * LANES * 4
    return pl.pallas_call(
        functools.partial(_s5_kernel, tl=tl, chain=chain, emit_y=emit_y),
        grid=(nblk,),
        in_specs=in_specs, out_specs=out_specs, out_shape=out_shape,
        scratch_shapes=scratch,
        compiler_params=_cparams(("arbitrary",),
                                 io_bytes + 3 * tl * n_seq * 2 * sb * 4 + (12 << 20)),
        name="s5_scan",
    )(*args)


def _glu_kernel(y_ref, w_ref, o_ref):
    y = y_ref[...]
    z = jnp.dot(y.astype(BF16), w_ref[...], preferred_element_type=F32)
    o_ref[...] = (y * jax.nn.sigmoid(z)).astype(o_ref.dtype)


def _glu(y, w, tm):
    m, k = y.shape
    return pl.pallas_call(
        _glu_kernel,
        grid=(m // tm,),
        in_specs=[pl.BlockSpec((tm, k), lambda i: (i, 0)),
                  pl.BlockSpec((k, k), lambda i: (0, 0))],
        out_specs=pl.BlockSpec((tm, k), lambda i: (i, 0)),
        out_shape=jax.ShapeDtypeStruct((m, k), BF16),
        compiler_params=_cparams(("arbitrary",), 2 * k * k * 2 + 6 * tm * k * 4 + (8 << 20)),
        name="glu",
    )(y, w)


def _out_proj_kernel(a1_ref, a2_ref, w1_ref, w2_ref, x_ref, g_ref, o_ref):
    acc = jnp.dot(a1_ref[...], w1_ref[...], preferred_element_type=F32)
    acc = acc + jnp.dot(a2_ref[...], w2_ref[...], preferred_element_type=F32)
    o_ref[...] = x_ref[...] + g_ref[...] * acc


def _out_proj(a1, a2, w, x, gate, tm, tn):
    bsz, t, d = x.shape
    k1, k2 = a1.shape[-1], a2.shape[-1]
    a1 = a1.reshape(bsz, t, k1)
    a2 = a2.reshape(bsz, t, k2)
    if gate.shape[1] == 1:
        g_spec = pl.BlockSpec((None, 1, tn), lambda b, i, j: (b, 0, j))
    else:
        g_spec = pl.BlockSpec((None, tm, tn), lambda b, i, j: (b, i, j))
    return pl.pallas_call(
        _out_proj_kernel,
        grid=(bsz, t // tm, d // tn),
        in_specs=[pl.BlockSpec((None, tm, k1), lambda b, i, j: (b, i, 0)),
                  pl.BlockSpec((None, tm, k2), lambda b, i, j: (b, i, 0)),
                  pl.BlockSpec((k1, tn), lambda b, i, j: (0, j)),
                  pl.BlockSpec((k2, tn), lambda b, i, j: (k1 // k2, j)),
                  pl.BlockSpec((None, tm, tn), lambda b, i, j: (b, i, j)),
                  g_spec],
        out_specs=pl.BlockSpec((None, tm, tn), lambda b, i, j: (b, i, j)),
        out_shape=jax.ShapeDtypeStruct((bsz, t, d), F32),
        compiler_params=_cparams(("arbitrary", "arbitrary", "arbitrary"),
                                 2 * (tm * (k1 + k2) * 2 + (k1 + k2) * tn * 2 + 3 * tm * tn * 4)
                                 + (8 << 20)),
        name="out_proj",
    )(a1, a2, w, w, x, gate)


def _top_rows(s, k):
    n_rows = s.shape[0]
    row = lax.broadcasted_iota(jnp.int32, s.shape, 0).astype(F32)
    k_iota = lax.broadcasted_iota(jnp.int32, (k, s.shape[1]), 0)
    rank = jnp.full(s.shape, NOT_SELECTED, F32)
    val_mat = jnp.zeros((k, s.shape[1]), F32)
    vals, idxs = [], []
    for kk in range(k):
        mx = jnp.max(s, axis=0, keepdims=True)
        idx = jnp.min(jnp.where(s == mx, row, float(n_rows)), axis=0, keepdims=True)
        hit = row == idx
        rank = jnp.where(hit, float(kk), rank)
        s = jnp.where(hit, -jnp.inf, s)
        val_mat = jnp.where(k_iota == kk, mx, val_mat)
        vals.append(mx)
        idxs.append(idx)
    return vals, val_mat, idxs, rank


def _peer_route_kernel(q_ref, keys_ref, p0_ref, th_ref, p1_ref, r1_ref):
    s0 = _dot_nt(keys_ref[0], q_ref[:, :PEER_DHALF])
    s1 = _dot_nt(keys_ref[1], q_ref[:, PEER_DHALF:])
    v0, _, _, rank0 = _top_rows(s0, PEER_TOPK)
    v1, sv1, _, rank1 = _top_rows(s1, PEER_TOPK)
    tm = s0.shape[1]
    n_b = [PEER_TOPK // (a + 1) for a in range(PEER_TOPK)]
    n_cand = sum(n_b)
    pad = (-n_cand) % SUBLANES
    parts = [v0[a] + sv1[:n_b[a]] for a in range(PEER_TOPK)]
    if pad:
        parts.append(jnp.full((pad, tm), -jnp.inf, F32))
    top_s, _, _, cand_rank = _top_rows(jnp.concatenate(parts, axis=0), PEER_TOPK)
    taken = jnp.where(cand_rank < float(PEER_TOPK), 1.0, 0.0)
    z = jnp.zeros((1, tm), F32)
    for kk in range(PEER_TOPK):
        z = z + jnp.exp(top_s[kk] - top_s[0])
    th = jnp.zeros(s0.shape, F32)
    start = 0
    for a in range(PEER_TOPK):
        n_sel = jnp.sum(taken[start:start + n_b[a]], axis=0, keepdims=True)
        th = jnp.where(rank0 == float(a), n_sel, th)
        start += n_b[a]
    p0_ref[...] = jnp.exp(s0 - v0[0])
    th_ref[...] = th
    p1_ref[...] = (jnp.exp(s1 - v1[0]) / z).astype(p1_ref.dtype)
    r1_ref[...] = rank1.astype(r1_ref.dtype)


def _peer_route(q, keys, tm):
    n = q.shape[0]
    dk = 2 * PEER_DHALF
    spec_out = pl.BlockSpec((None, PEER_NKEYS, tm), lambda i, h: (h, 0, i))
    shapes_out = [jax.ShapeDtypeStruct((PEER_HEADS, PEER_NKEYS, n), dt)
                  for dt in (F32, F32, BF16, BF16)]
    return pl.pallas_call(
        _peer_route_kernel,
        grid=(n // tm, PEER_HEADS),
        in_specs=[pl.BlockSpec((tm, dk), lambda i, h: (i, h)),
                  pl.BlockSpec((None, 2, PEER_NKEYS, PEER_DHALF), lambda i, h: (h, 0, 0, 0))],
        out_specs=[spec_out] * 4,
        out_shape=shapes_out,
        compiler_params=_cparams(("arbitrary", "arbitrary"),
                                 4 * tm * dk * 2 + 48 * PEER_NKEYS * tm * 4 + (8 << 20)),
        name="peer_route",
    )(q, keys)


BF16_ROWS = 2 * SUBLANES


def _rows_bf16(row, n_rows):
    tile = jnp.broadcast_to(row, (BF16_ROWS, row.shape[1])).astype(BF16)
    return jnp.concatenate([tile] * (n_rows // BF16_ROWS), axis=0)


def _peer_dense_kernel(h_ref, u_ref, vt_ref, p0_ref, th_ref, p1_ref, r1_ref, o_ref, w_ref, *, te):
    e = pl.program_id(1)

    @pl.when(e == 0)
    def _():
        o_ref[...] = jnp.zeros(o_ref.shape, F32)

    for ib in range(te // PEER_NKEYS):
        i0 = e * (te // PEER_NKEYS) + ib
        th_rows = [th_ref[h, pl.ds(i0, 1), :] for h in range(PEER_HEADS)]
        p0_rows = [p0_ref[h, pl.ds(i0, 1), :] for h in range(PEER_HEADS)]
        for c in range(w_ref.shape[1] // LANES):
            tok = slice(c * LANES, (c + 1) * LANES)
            w = jnp.zeros((PEER_NKEYS, LANES), BF16)
            for h in range(PEER_HEADS):
                th = _rows_bf16(th_rows[h][:, tok], PEER_NKEYS)
                p0 = _rows_bf16(p0_rows[h][:, tok], PEER_NKEYS)
                w = w + jnp.where(r1_ref[h, :, tok] < th, p1_ref[h, :, tok] * p0,
                                  jnp.zeros_like(w))
            w_ref[ib * PEER_NKEYS:(ib + 1) * PEER_NKEYS, tok] = w

    act = jax.nn.gelu(_dot_nt(u_ref[...], h_ref[...]))
    coef = (w_ref[...].astype(F32) * act).astype(BF16)
    o_ref[...] += jnp.dot(vt_ref[...], coef, preferred_element_type=F32)


def _peer_dense(h2, u, v_t, sel, tm, te):
    n, d = h2.shape
    n_exp = u.shape[0]
    once = pl.Buffered(1)
    sel_spec = pl.BlockSpec((PEER_HEADS, PEER_NKEYS, tm), lambda i, e: (0, 0, i),
                            pipeline_mode=once)
    return pl.pallas_call(
        functools.partial(_peer_dense_kernel, te=te),
        grid=(n // tm, n_exp // te),
        in_specs=[pl.BlockSpec((tm, d), lambda i, e: (i, 0), pipeline_mode=once),
                  pl.BlockSpec((te, d), lambda i, e: (e, 0)),
                  pl.BlockSpec((d, te), lambda i, e: (0, e))] + [sel_spec] * 4,
        out_specs=pl.BlockSpec((d, tm), lambda i, e: (0, i), pipeline_mode=once),
        out_shape=jax.ShapeDtypeStruct((d, n), F32),
        scratch_shapes=[pltpu.VMEM((te, tm), BF16)],
        compiler_params=_cparams(
            ("arbitrary", "arbitrary"),
            tm * d * 2 + 2 * 2 * te * d * 2 + 4 * PEER_HEADS * PEER_NKEYS * tm * 4 + d * tm * 4
            + 8 * te * tm * 4 + (6 << 20)),
        name="peer_dense",
    )(h2, u, v_t, *sel)


def _peer_out_kernel(ot_ref, x_ref, g_ref, fg_ref, y_ref):
    x2 = x_ref[...] + g_ref[...] * ot_ref[...].T
    y_ref[...] = _rms(x2) * fg_ref[...]


def _peer_out(o_t, x, gate, final_g, tm):
    bsz, t, d = x.shape
    nt = t // tm
    return pl.pallas_call(
        _peer_out_kernel,
        grid=(bsz, nt),
        in_specs=[pl.BlockSpec((d, tm), lambda b, i: (0, b * nt + i)),
                  pl.BlockSpec((None, tm, d), lambda b, i: (b, i, 0)),
                  _mod_spec(gate, tm, d),
                  pl.BlockSpec((1, d), lambda b, i: (0, 0))],
        out_specs=pl.BlockSpec((None, tm, d), lambda b, i: (b, i, 0)),
        out_shape=jax.ShapeDtypeStruct((bsz, t, d), F32),
        compiler_params=_cparams(("arbitrary", "arbitrary"), 10 * tm * d * 4 + (8 << 20)),
        name="peer_out",
    )(o_t, x, gate, final_g.reshape(1, d))


def _alibi_slopes(n):
    return tuple(float(np.float32(2.0 ** (-8.0 * (h + 1) / n))) for h in range(n))


def _s5_params(lam_re, lam_im, log_dt, b_re, b_im, c_re, c_im, d_skip, seg_len):
    dt = jnp.exp(log_dt.astype(F32))[:, None]
    lr, li = lam_re.astype(F32), lam_im.astype(F32)
    mag = jnp.exp(lr * dt)
    a_re, a_im = mag * jnp.cos(li * dt), mag * jnp.sin(li * dt)
    den = lr * lr + li * li
    nr, ni = a_re - 1.0, a_im
    f_re, f_im = (nr * lr + ni * li) / den, (ni * lr - nr * li) / den
    br, bi = b_re.astype(F32), b_im.astype(F32)
    bb_re = f_re[..., None] * br - f_im[..., None] * bi
    bb_im = f_re[..., None] * bi + f_im[..., None] * br
    mag_s = jnp.exp(lr * dt * seg_len)
    s_re, s_im = mag_s * jnp.cos(li * dt * seg_len), mag_s * jnp.sin(li * dt * seg_len)

    g = lam_re.shape[0]
    nblk = g // GROUPS_PER_BLOCK
    eye = jnp.eye(GROUPS_PER_BLOCK, dtype=F32)

    def in_blocks(w):
        w = w.reshape(nblk, GROUPS_PER_BLOCK, SSM_STATE, SSM_GROUP)
        return jnp.einsum('jgnc,gk->jgckn', w, eye).reshape(nblk, LANES, STATE_BLOCK)

    def out_blocks(w):
        w = w.reshape(nblk, GROUPS_PER_BLOCK, SSM_GROUP, SSM_STATE)
        return jnp.einsum('jgcn,gk->jgnkc', w, eye).reshape(nblk, STATE_BLOCK, LANES)

    bb_blk = jnp.concatenate([in_blocks(bb_re), in_blocks(bb_im)], axis=2).astype(BF16)
    c_blk = jnp.concatenate([out_blocks(c_re.astype(F32)), -out_blocks(c_im.astype(F32))],
                            axis=1).astype(BF16)
    a_blk = jnp.stack([a_re.reshape(nblk, STATE_BLOCK), a_im.reshape(nblk, STATE_BLOCK)], axis=1)
    aseg_blk = jnp.stack([s_re.reshape(nblk, STATE_BLOCK), s_im.reshape(nblk, STATE_BLOCK)], axis=1)
    d_blk = d_skip.astype(F32).reshape(nblk, 1, LANES)
    return bb_blk, a_blk, aseg_blk, c_blk, d_blk


def _pick(n, pref):
    t = min(n, pref)
    while n % t:
        t //= 2
    return t


def kernel(x_prompt, x_sample, cache_k, cache_v, page_table, state_ssm_re, state_ssm_im,
           c_prompt, c_sample, w_ada, b_ada, norm1_g, norm2_g, w_in,
           lam_q1, lam_k1, lam_q2, lam_k2, subln_g,
           ssm_lam_re, ssm_lam_im, ssm_log_dt, ssm_b_re, ssm_b_im, ssm_c_re, ssm_c_im, ssm_d,
           w_glu, w_out, peer_wq, peer_sub_keys, peer_u, peer_v, final_g):
    depth = w_ada.shape[0]
    assert depth == 1, "kernel() fuses the final norm into the layer and supports DEPTH == 1"
    bsz, seq, d_model = x_prompt.shape
    dec_b, dec_t, _ = x_sample.shape
    n_p, n_s = bsz * seq, dec_b * dec_t
    n_pages, page = page_table.shape[1], cache_k.shape[2]
    past_len = n_pages * page
    attn_w = ATTN_HEADS * ATTN_DV
    ssm_w = w_in.shape[2] - 3 * attn_w
    n_groups = ssm_w // SSM_GROUP
    slopes = _alibi_slopes(ATTN_HEADS)
    seg_len = seq // PROMPT_SEGMENTS
    n_seq_p = bsz * PROMPT_SEGMENTS

    xp = x_prompt
    xs = x_sample.reshape(1, n_s, d_model)
    outs = [[] for _ in range(8)]

    for l in range(depth):
        lam_init = 0.8 - 0.6 * math.exp(-0.3 * l)
        out_scale = 1.0 - lam_init
        lam = (jnp.exp(jnp.sum(lam_q1[l].astype(F32) * lam_k1[l].astype(F32)))
               - jnp.exp(jnp.sum(lam_q2[l].astype(F32) * lam_k2[l].astype(F32))) + lam_init)
        scalars = jnp.concatenate([lam.reshape(1), jnp.asarray(slopes, F32)])

        c_all = jnp.concatenate([c_prompt, c_sample], axis=0)
        pad = (-c_all.shape[0]) % SUBLANES
        c_all = jnp.pad(c_all, ((0, pad), (0, 0)))
        mod = _adaln(c_all, w_ada[l], b_ada[l])
        mod_p = [m[:, None, :] for m in jnp.split(mod[:bsz], N_MOD, axis=-1)]
        mod_s = [jnp.repeat(m, dec_t, axis=0)[None] for m in
                 jnp.split(mod[bsz:bsz + dec_b], N_MOD, axis=-1)]

        w_in_b = w_in[l].astype(BF16)
        w_glu_b = w_glu[l].astype(BF16)
        w_out_b = w_out[l].astype(BF16)
        wq_b = peer_wq[l].astype(BF16)
        keys_b = peer_sub_keys[l].astype(BF16)
        u_b = peer_u[l].astype(BF16)
        v_t = peer_v[l].T.astype(BF16)
        bb_blk, a_blk, aseg_blk, c_blk, d_blk = _s5_params(
            ssm_lam_re[l], ssm_lam_im[l], ssm_log_dt[l], ssm_b_re[l], ssm_b_im[l],
            ssm_c_re[l], ssm_c_im[l], ssm_d[l], seg_len)

        def front(x, mods, tm):
            sh1, sc1 = mods[0], mods[1]
            h = _norm_mod(x, norm1_g[l], sh1, sc1, tm).reshape(-1, d_model)
            tmm = _pick(h.shape[0], 2048)
            (q_b, q_f) = _matmul_cols(h, w_in_b, 0, attn_w, (BF16, F32), tmm, 512)
            (k_f, k_b) = _matmul_cols(h, w_in_b, attn_w, attn_w, (F32, BF16), tmm, 512)
            (v_f, v_b) = _matmul_cols(h, w_in_b, 2 * attn_w, attn_w, (F32, BF16), tmm, 512)
            (u_f,) = _matmul_cols(h, w_in_b, 3 * attn_w, ssm_w, (F32,), tmm, 512)
            return q_b, q_f, k_f, k_b, v_f, v_b, u_f

        def back(x, mods, o_attn, o_ssm, tm):
            g1, sh2, sc2, g2 = mods[2], mods[3], mods[4], mods[5]
            n = x.shape[0] * x.shape[1]
            x1 = _out_proj(o_attn, o_ssm, w_out_b, x, g1, _pick(x.shape[1], 1024), 512)
            h2 = _norm_mod(x1, norm2_g[l], sh2, sc2, tm).reshape(n, d_model)
            tmp = _pick(n, 512)
            (q_peer,) = _matmul_cols(h2, wq_b, 0, wq_b.shape[1], (BF16,), _pick(n, 2048), 512)
            sel = _peer_route(q_peer, keys_b, tmp)
            o_t = _peer_dense(h2, u_b, v_t, sel, tmp, 512)
            return _peer_out(o_t, x1, g2, final_g, _pick(x.shape[1], 256))

        q_b, _, k_f, k_b, v_f, v_b, u_f = front(xp, mod_p, 256)
        o_attn_p = _prompt_attention(scalars, q_b, k_b, v_b, subln_g[l], bsz, seq, out_scale,
                                     256, 8)
        u_tm = u_f.reshape(bsz, PROMPT_SEGMENTS, seg_len, ssm_w).transpose(2, 0, 1, 3)
        u_tm = u_tm.reshape(seg_len, n_seq_p, ssm_w)
        zeros_p = jnp.zeros((2, n_seq_p, n_groups * SSM_STATE), F32)
        (seg_end,) = _s5_scan(u_tm, bb_blk, a_blk, zeros_p, None, None, None,
                              tl=16, chain=False, emit_y=False)
        y_tm, h_end = _s5_scan(u_tm, bb_blk, a_blk, seg_end, aseg_blk, c_blk, d_blk,
                               tl=16, chain=True, emit_y=True)
        y_p = y_tm.reshape(seg_len, bsz, PROMPT_SEGMENTS, ssm_w).transpose(1, 2, 0, 3)
        o_ssm_p = _glu(y_p.reshape(n_p, ssm_w), w_glu_b, 256)
        y_prompt_l = back(xp, mod_p, o_attn_p, o_ssm_p, 256)
        h_end = h_end.reshape(2, bsz, PROMPT_SEGMENTS, n_groups, SSM_STATE)[:, :, -1]
        outs[0].append(k_f.reshape(bsz, seq, ATTN_HEADS, ATTN_DV))
        outs[1].append(v_f.reshape(bsz, seq, ATTN_HEADS, ATTN_DV))
        outs[4].append(h_end[0])
        outs[5].append(h_end[1])

        _, q_f, k_f, _, v_f, _, u_f = front(xs, mod_s, n_s)
        qh = q_f.reshape(dec_b, dec_t, ATTN_HEADS, ATTN_DV).transpose(0, 2, 1, 3)
        lane = jnp.arange(ATTN_DV) < ATTN_DH
        qh = jnp.stack([jnp.where(lane, qh, 0.0), jnp.where(lane, 0.0, qh)], axis=2)
        qh = jnp.pad(qh, ((0, 0), (0, 0), (0, 0), (0, SUBLANES - dec_t), (0, 0)))
        q_rows = (qh * (ATTN_DH ** -0.5)).astype(BF16).reshape(
            dec_b, ATTN_HEADS * ROWS_PER_HEAD, ATTN_DV)
        kv_view = (-1, page, HEAD_TILES, SUBLANES, ATTN_DV)
        new_pad = ((0, 0), (0, page - dec_t), (0, 0), (0, 0), (0, 0))
        k_new = jnp.pad(k_f.reshape((dec_b, dec_t) + kv_view[2:]), new_pad)
        v_new = jnp.pad(v_f.reshape((dec_b, dec_t) + kv_view[2:]), new_pad)
        o_rows = _decode_attention(
            page_table, scalars, q_rows, cache_k.reshape(kv_view), cache_v.reshape(kv_view),
            k_new, v_new, subln_g[l], page, past_len, out_scale, slopes, 8)
        o_attn_s = o_rows.reshape(dec_b, ATTN_HEADS, SUBLANES, ATTN_DV)[:, :, :dec_t]
        o_attn_s = o_attn_s.transpose(0, 2, 1, 3).reshape(n_s, attn_w).astype(BF16)
        u_tm = u_f.reshape(dec_b, dec_t, ssm_w).transpose(1, 0, 2)
        h0 = jnp.stack([state_ssm_re[l].reshape(dec_b, -1), state_ssm_im[l].reshape(dec_b, -1)])
        y_tm, h_end = _s5_scan(u_tm, bb_blk, a_blk, h0.astype(F32), None, c_blk, d_blk,
                               tl=dec_t, chain=False, emit_y=True)
        o_ssm_s = _glu(y_tm.transpose(1, 0, 2).reshape(n_s, ssm_w), w_glu_b, n_s)
        y_sample_l = back(xs, mod_s, o_attn_s, o_ssm_s, n_s)
        outs[2].append(k_f.reshape(dec_b, dec_t, ATTN_HEADS, ATTN_DV))
        outs[3].append(v_f.reshape(dec_b, dec_t, ATTN_HEADS, ATTN_DV))
        outs[6].append(h_end[0].reshape(dec_b, n_groups, SSM_STATE))
        outs[7].append(h_end[1].reshape(dec_b, n_groups, SSM_STATE))

    y_prompt = y_prompt_l
    y_sample = y_sample_l.reshape(dec_b, dec_t, d_model)
    return (y_prompt, y_sample) + tuple(jnp.stack(o) for o in outs)
```

```python
import functools
import math

import numpy as np
import jax
import jax.numpy as jnp
from jax import lax
from jax.experimental import pallas as pl
from jax.experimental.pallas import tpu as pltpu

F32 = jnp.float32
BF16 = jnp.bfloat16

V7X_VMEM_BYTES = 64 * 1024 * 1024
LANES = 128
SUBLANES = 8

EPS = 1e-6
NEG_INF = -1e30

ATTN_HEADS = 16
ATTN_DH = 64
ATTN_DV = 2 * ATTN_DH
SSM_GROUP = 16
SSM_STATE = 64
GROUPS_PER_BLOCK = LANES // SSM_GROUP
STATE_BLOCK = GROUPS_PER_BLOCK * SSM_STATE
PROMPT_SEGMENTS = 8
PEER_HEADS = 8
PEER_NKEYS = 128
PEER_DHALF = 128
PEER_TOPK = 16
N_MOD = 6
NOT_SELECTED = 99.0


def _cparams(semantics, vmem_bytes):
    return pltpu.CompilerParams(dimension_semantics=semantics,
                                vmem_limit_bytes=min(int(vmem_bytes), V7X_VMEM_BYTES - (4 << 20)))


def _dot_nt(a, b):
    return lax.dot_general(a, b, (((1,), (1,)), ((), ())), preferred_element_type=F32)


def _rms(x):
    return x * lax.rsqrt(jnp.mean(x * x, axis=-1, keepdims=True) + EPS)


def _adaln_kernel(c_ref, w_ref, b_ref, o_ref):
    a = jax.nn.silu(c_ref[...])
    o_ref[...] = jnp.dot(a.astype(BF16), w_ref[...].astype(BF16),
                         preferred_element_type=F32) + b_ref[...]


def _adaln(c, w, b, tn=1024):
    m, k = c.shape
    n = w.shape[1]
    return pl.pallas_call(
        _adaln_kernel,
        grid=(n // tn,),
        in_specs=[pl.BlockSpec((m, k), lambda j: (0, 0)),
                  pl.BlockSpec((k, tn), lambda j: (0, j)),
                  pl.BlockSpec((1, tn), lambda j: (0, j))],
        out_specs=pl.BlockSpec((m, tn), lambda j: (0, j)),
        out_shape=jax.ShapeDtypeStruct((m, n), F32),
        compiler_params=_cparams(("arbitrary",), 2 * k * tn * 4 + (8 << 20)),
        name="adaln",
    )(c, w, b.reshape(1, n))


def _norm_mod_kernel(x_ref, g_ref, sh_ref, sc_ref, o_ref):
    y = _rms(x_ref[...]) * g_ref[...]
    o_ref[...] = (y * (1.0 + sc_ref[...]) + sh_ref[...]).astype(o_ref.dtype)


def _mod_spec(mod, tm, d):
    if mod.shape[1] == 1:
        return pl.BlockSpec((None, 1, d), lambda b, i: (b, 0, 0))
    return pl.BlockSpec((None, tm, d), lambda b, i: (b, i, 0))


def _norm_mod(x, g, shift, scale, tm):
    bsz, t, d = x.shape
    return pl.pallas_call(
        _norm_mod_kernel,
        grid=(bsz, t // tm),
        in_specs=[pl.BlockSpec((None, tm, d), lambda b, i: (b, i, 0)),
                  pl.BlockSpec((1, d), lambda b, i: (0, 0)),
                  _mod_spec(shift, tm, d), _mod_spec(scale, tm, d)],
        out_specs=pl.BlockSpec((None, tm, d), lambda b, i: (b, i, 0)),
        out_shape=jax.ShapeDtypeStruct((bsz, t, d), BF16),
        compiler_params=_cparams(("arbitrary", "arbitrary"), 8 * tm * d * 4 + (8 << 20)),
        name="norm_mod",
    )(x, g.reshape(1, d), shift, scale)


def _mm_kernel(a_ref, b_ref, *o_refs):
    acc = jnp.dot(a_ref[...], b_ref[...], preferred_element_type=F32)
    for o_ref in o_refs:
        o_ref[...] = acc.astype(o_ref.dtype)


def _matmul_cols(a, b, col0, ncols, out_dtypes, tm, tn):
    m, k = a.shape
    joff = col0 // tn
    out_bytes = sum(jnp.dtype(dt).itemsize for dt in out_dtypes)
    return pl.pallas_call(
        _mm_kernel,
        grid=(m // tm, ncols // tn),
        in_specs=[pl.BlockSpec((tm, k), lambda i, j: (i, 0)),
                  pl.BlockSpec((k, tn), lambda i, j: (0, j + joff))],
        out_specs=[pl.BlockSpec((tm, tn), lambda i, j: (i, j)) for _ in out_dtypes],
        out_shape=[jax.ShapeDtypeStruct((m, ncols), dt) for dt in out_dtypes],
        compiler_params=_cparams(("arbitrary", "arbitrary"),
                                 2 * (tm * k * 2 + k * tn * 2 + tm * tn * out_bytes)
                                 + tm * tn * 4 + (8 << 20)),
        name="matmul_cols",
    )(a, b)


def _prompt_attn_kernel(sc_ref, q_ref, k_ref, v_ref, g_ref, o_ref, m_ref, l_ref, acc_ref,
                        *, tq, hp, out_scale):
    hg = pl.program_id(1)
    qi = pl.program_id(2)
    lam = sc_ref[0]
    slopes = [sc_ref[1 + hg * hp + j] for j in range(hp)]

    def head_cols(j):
        return slice(j * ATTN_DV, (j + 1) * ATTN_DV)

    qqs = []
    for j in range(hp):
        q = q_ref[:, head_cols(j)] * jnp.asarray(ATTN_DH ** -0.5, BF16)
        lane = lax.broadcasted_iota(jnp.int32, q.shape, 1)
        zero = jnp.zeros_like(q)
        qqs.append(jnp.concatenate([jnp.where(lane < ATTN_DH, q, zero),
                                    jnp.where(lane >= ATTN_DH, q, zero)], axis=0))

    m_ref[...] = jnp.full(m_ref.shape, NEG_INF, F32)
    l_ref[...] = jnp.zeros(l_ref.shape, F32)
    acc_ref[...] = jnp.zeros(acc_ref.shape, F32)
    col = lax.broadcasted_iota(jnp.int32, (1, tq), 1)
    lane_reps = tq // LANES

    def block(kj, masked):
        off = pl.multiple_of(kj * tq, tq)
        rel = (col + (kj - qi) * tq).astype(F32)
        for j in range(hp):
            k = k_ref[pl.ds(off, tq), head_cols(j)]
            v = v_ref[pl.ds(off, tq), head_cols(j)]
            s = _dot_nt(qqs[j], k) + slopes[j] * rel
            if masked:
                row = lax.broadcasted_iota(jnp.int32, s.shape, 0)
                row = jnp.where(row >= tq, row - tq, row)
                s = jnp.where(row >= lax.broadcasted_iota(jnp.int32, s.shape, 1), s, NEG_INF)
            m_old = m_ref[j]
            m_new = jnp.maximum(m_old, jnp.max(s, axis=-1, keepdims=True))
            alpha = jnp.exp(m_old - m_new)
            p = jnp.exp(s - jnp.concatenate([m_new] * lane_reps, axis=1))
            l_ref[j] = alpha * l_ref[j] + jnp.sum(p, axis=-1, keepdims=True)
            acc_ref[j] = alpha * acc_ref[j] + jnp.dot(p.astype(BF16), v,
                                                      preferred_element_type=F32)
            m_ref[j] = m_new

    def full_block(kj, carry):
        block(kj, False)
        return carry

    lax.fori_loop(0, qi, full_block, 0)
    block(qi, True)

    for j in range(hp):
        o = acc_ref[j] / l_ref[j]
        o = o[:tq] - lam * o[tq:]
        o_ref[:, head_cols(j)] = (_rms(o) * g_ref[...] * out_scale).astype(o_ref.dtype)


def _prompt_attention(scalars, q, k, v, subln_g, bsz, t, out_scale, tq, hp):
    n, width = q.shape
    nq = t // tq
    assert ATTN_DV == LANES and tq % LANES == 0 and ATTN_HEADS % hp == 0
    wb = hp * ATTN_DV
    return pl.pallas_call(
        functools.partial(_prompt_attn_kernel, tq=tq, hp=hp, out_scale=out_scale),
        grid=(bsz, ATTN_HEADS // hp, nq),
        in_specs=[pl.BlockSpec(memory_space=pltpu.SMEM),
                  pl.BlockSpec((tq, wb), lambda b, h, i: (b * nq + i, h)),
                  pl.BlockSpec((t, wb), lambda b, h, i: (b, h)),
                  pl.BlockSpec((t, wb), lambda b, h, i: (b, h)),
                  pl.BlockSpec((1, ATTN_DV), lambda b, h, i: (0, 0))],
        out_specs=pl.BlockSpec((tq, wb), lambda b, h, i: (b * nq + i, h)),
        out_shape=jax.ShapeDtypeStruct((n, width), BF16),
        scratch_shapes=[pltpu.VMEM((hp, 2 * tq, LANES), F32), pltpu.VMEM((hp, 2 * tq, LANES), F32),
                        pltpu.VMEM((hp, 2 * tq, ATTN_DV), F32)],
        compiler_params=_cparams(("arbitrary", "arbitrary", "arbitrary"),
                                 8 * t * wb * 2 + 16 * hp * tq * tq * 4 + (8 << 20)),
        name="prompt_attn",
    )(scalars, q, k, v, subln_g.reshape(1, ATTN_DV))


ROWS_PER_HEAD = 2 * SUBLANES
HEAD_TILES = ATTN_HEADS // SUBLANES


def _decode_attn_kernel(pt_ref, sc_ref, q_ref, slope_ref, *refs, n_steps, pps, page, past_len,
                        out_scale):
    del pt_ref
    nt = HEAD_TILES
    kc_refs = [refs[i * nt:(i + 1) * nt] for i in range(pps)]
    vc_refs = [refs[(pps + i) * nt:(pps + i + 1) * nt] for i in range(pps)]
    rest = refs[2 * pps * nt:]
    kn_refs, vn_refs = rest[:nt], rest[nt:2 * nt]
    g_ref, o_ref, m_ref, l_ref, acc_ref = rest[2 * nt:]
    p = pl.program_id(1)

    @pl.when(p == 0)
    def _():
        m_ref[...] = jnp.full(m_ref.shape, NEG_INF, F32)
        l_ref[...] = jnp.zeros(l_ref.shape, F32)
        acc_ref[...] = jnp.zeros(acc_ref.shape, F32)

    def head_rows(h):
        return slice(h * ROWS_PER_HEAD, (h + 1) * ROWS_PER_HEAD)

    def head_keys(page_refs, h):
        parts = [tiles[h // SUBLANES].reshape(page * SUBLANES, ATTN_DV)[
            pl.ds(h % SUBLANES, page, stride=SUBLANES), :].astype(BF16) for tiles in page_refs]
        return parts[0] if len(parts) == 1 else jnp.concatenate(parts, axis=0)

    def lane_tile(x, reps):
        return x if reps == 1 else jnp.concatenate([x] * reps, axis=1)

    def process(k_refs, v_refs, base_pos, causal):
        reps = len(k_refs)
        col = lax.broadcasted_iota(jnp.int32, (1, reps * page), 1)
        rel = (col + base_pos).astype(F32)
        s = jnp.concatenate([_dot_nt(q_ref[head_rows(h), :], head_keys(k_refs, h))
                             for h in range(ATTN_HEADS)], axis=0)
        s = s + lane_tile(slope_ref[...], reps) * rel
        if causal:
            t = lax.broadcasted_iota(jnp.int32, s.shape, 0) & (SUBLANES - 1)
            s = jnp.where(lax.broadcasted_iota(jnp.int32, s.shape, 1) <= t, s, NEG_INF)
        m_old = m_ref[...]
        m_new = jnp.maximum(m_old, jnp.max(s, axis=-1, keepdims=True))
        alpha = jnp.exp(m_old - m_new)
        e = jnp.exp(s - lane_tile(m_new, reps))
        l_ref[...] = alpha * l_ref[...] + jnp.sum(e, axis=-1, keepdims=True)
        eb = e.astype(BF16)
        pv = jnp.concatenate([jnp.dot(eb[head_rows(h), :], head_keys(v_refs, h),
                                      preferred_element_type=F32)
                              for h in range(ATTN_HEADS)], axis=0)
        acc_ref[...] = alpha * acc_ref[...] + pv
        m_ref[...] = m_new

    @pl.when(p < n_steps)
    def _():
        process(kc_refs, vc_refs, p * (pps * page) - past_len, False)

    @pl.when(p == n_steps)
    def _():
        process([kn_refs], [vn_refs], 0, True)
        lam = sc_ref[0]
        o = acc_ref[...] / l_ref[...]
        for h in range(ATTN_HEADS):
            r0 = h * ROWS_PER_HEAD
            oh = o[r0:r0 + SUBLANES] - lam * o[r0 + SUBLANES:r0 + ROWS_PER_HEAD]
            o_ref[h * SUBLANES:(h + 1) * SUBLANES, :] = _rms(oh) * g_ref[...] * out_scale


def _decode_attention(page_table, scalars, q_rows, cache_k, cache_v, k_new, v_new, subln_g,
                      page, past_len, out_scale, slopes, pps):
    bsz, n_pages = page_table.shape
    rows = ATTN_HEADS * ROWS_PER_HEAD
    assert cache_k.shape[1:] == (page, HEAD_TILES, SUBLANES, ATTN_DV)
    assert page == ATTN_DV == LANES and n_pages % pps == 0
    page_rows = page * ATTN_HEADS
    n_steps = n_pages // pps
    slope_rows = jnp.broadcast_to(
        jnp.repeat(jnp.asarray(slopes, F32), ROWS_PER_HEAD)[:, None], (rows, page))
    half_block = (None, page, None, SUBLANES, ATTN_DV)

    def cache_spec(slot, tile):
        def index_map(b, p, pt):
            return (pt[b, jnp.minimum(p, n_steps - 1) * pps + slot], 0, tile, 0, 0)
        return pl.BlockSpec(half_block, index_map)

    def new_spec(tile):
        return pl.BlockSpec(half_block, lambda b, p, pt: (b, 0, tile, 0, 0))

    cache_specs = [cache_spec(i, t) for i in range(pps) for t in range(HEAD_TILES)]
    new_specs = [new_spec(t) for t in range(HEAD_TILES)]
    grid_spec = pltpu.PrefetchScalarGridSpec(
        num_scalar_prefetch=1,
        grid=(bsz, n_steps + 1),
        in_specs=[pl.BlockSpec(memory_space=pltpu.SMEM),
                  pl.BlockSpec((None, rows, ATTN_DV), lambda b, p, pt: (b, 0, 0)),
                  pl.BlockSpec((rows, page), lambda b, p, pt: (0, 0))]
                 + cache_specs * 2 + new_specs * 2
                 + [pl.BlockSpec((1, ATTN_DV), lambda b, p, pt: (0, 0))],
        out_specs=pl.BlockSpec((None, ATTN_HEADS * SUBLANES, ATTN_DV), lambda b, p, pt: (b, 0, 0)),
        scratch_shapes=[pltpu.VMEM((rows, page), F32), pltpu.VMEM((rows, page), F32),
                        pltpu.VMEM((rows, ATTN_DV), F32)],
    )
    return pl.pallas_call(
        functools.partial(_decode_attn_kernel, n_steps=n_steps, pps=pps, page=page,
                          past_len=past_len, out_scale=out_scale),
        grid_spec=grid_spec,
        out_shape=jax.ShapeDtypeStruct((bsz, ATTN_HEADS * SUBLANES, ATTN_DV), F32),
        compiler_params=_cparams(("arbitrary", "arbitrary"),
                                 (4 * pps + 4) * page_rows * ATTN_DV * 4 + (12 << 20)),
        name="decode_attn",
    )(page_table, scalars, q_rows, slope_rows,
      *([cache_k] * (pps * HEAD_TILES)), *([cache_v] * (pps * HEAD_TILES)),
      *([k_new] * HEAD_TILES), *([v_new] * HEAD_TILES), subln_g.reshape(1, ATTN_DV))


def _s5_kernel(*refs, tl, chain, emit_y):
    refs = list(refs)
    u_ref, bb_ref, a_ref, h0_ref = refs[:4]
    refs = refs[4:]
    if chain:
        aseg_ref = refs.pop(0)
    if emit_y:
        c_ref, d_ref = refs[:2]
        refs = refs[2:]
        y_ref = refs.pop(0)
    ht_ref = refs.pop(0)
    hs_ref = refs.pop(0)
    if chain:
        init_ref = refs.pop(0)

    n_steps, n_seq, _ = u_ref.shape
    sb = STATE_BLOCK
    ar = a_ref[0:1, :]
    ai = a_ref[1:2, :]

    if chain:
        seg_r = aseg_ref[0:1, :]
        seg_i = aseg_ref[1:2, :]
        for b in range(n_seq // PROMPT_SEGMENTS):
            r = jnp.zeros((1, sb), F32)
            i = jnp.zeros((1, sb), F32)
            for j in range(PROMPT_SEGMENTS):
                row = b * PROMPT_SEGMENTS + j
                init_ref[0, row:row + 1, :] = r
                init_ref[1, row:row + 1, :] = i
                er = h0_ref[0, row:row + 1, :]
                ei = h0_ref[1, row:row + 1, :]
                r, i = seg_r * r - seg_i * i + er, seg_r * i + seg_i * r + ei
        h_init = (init_ref[0], init_ref[1])
    else:
        h_init = (h0_ref[0], h0_ref[1])

    def chunk(c, carry):
        t0 = pl.multiple_of(c * tl, tl)
        x = u_ref[pl.ds(t0, tl)].reshape(tl * n_seq, LANES)
        hs_ref[...] = jnp.dot(x.astype(BF16), bb_ref[...], preferred_element_type=F32)

        def step(t, hc):
            hr, hi = hc
            r0 = pl.multiple_of(t * n_seq, n_seq)
            bur = hs_ref[pl.ds(r0, n_seq), 0:sb]
            bui = hs_ref[pl.ds(r0, n_seq), sb:2 * sb]
            nr = ar * hr - ai * hi + bur
            ni = ar * hi + ai * hr + bui
            if emit_y:
                hs_ref[pl.ds(r0, n_seq), 0:sb] = nr
                hs_ref[pl.ds(r0, n_seq), sb:2 * sb] = ni
            return nr, ni

        carry = lax.fori_loop(0, tl, step, carry, unroll=True)
        if emit_y:
            y = jnp.dot(hs_ref[...].astype(BF16), c_ref[...], preferred_element_type=F32)
            y = jax.nn.gelu(y + d_ref[...] * x)
            y_ref[pl.ds(t0, tl)] = y.reshape(tl, n_seq, LANES)
        return carry

    hr, hi = lax.fori_loop(0, n_steps // tl, chunk, h_init)
    ht_ref[0] = hr
    ht_ref[1] = hi


def _s5_scan(u_tm, bb_blk, a_blk, h0, aseg_blk, c_blk, d_blk, *, tl, chain, emit_y):
    n_steps, n_seq, width = u_tm.shape
    nblk = width // LANES
    sb = STATE_BLOCK
    in_specs = [pl.BlockSpec((n_steps, n_seq, LANES), lambda j: (0, 0, j)),
                pl.BlockSpec((None, LANES, 2 * sb), lambda j: (j, 0, 0)),
                pl.BlockSpec((None, 2, sb), lambda j: (j, 0, 0)),
                pl.BlockSpec((2, n_seq, sb), lambda j: (0, 0, j))]
    args = [u_tm, bb_blk, a_blk, h0]
    if chain:
        in_specs.append(pl.BlockSpec((None, 2, sb), lambda j: (j, 0, 0)))
        args.append(aseg_blk)
    out_specs = []
    out_shape = []
    if emit_y:
        in_specs += [pl.BlockSpec((None, 2 * sb, LANES), lambda j: (j, 0, 0)),
                     pl.BlockSpec((None, 1, LANES), lambda j: (j, 0, 0))]
        args += [c_blk, d_blk]
        out_specs.append(pl.BlockSpec((n_steps, n_seq, LANES), lambda j: (0, 0, j)))
        out_shape.append(jax.ShapeDtypeStruct((n_steps, n_seq, width), F32))
    out_specs.append(pl.BlockSpec((2, n_seq, sb), lambda j: (0, 0, j)))
    out_shape.append(jax.ShapeDtypeStruct((2, n_seq, nblk * sb), F32))
    scratch = [pltpu.VMEM((tl * n_seq, 2 * sb), F32)]
    if chain:
        scratch.append(pltpu.VMEM((2, n_seq, sb), F32))
    io_bytes = 2 * (1 + int(emit_y)) * n_steps * n_seq * LANES * 4
    return pl.pallas_call(
        functools.partial(_s5_kernel, tl=tl, chain=chain, emit_y=emit_y),
        grid=(nblk,),
        in_specs=in_specs, out_specs=out_specs, out_shape=out_shape,
        scratch_shapes=scratch,
        compiler_params=_cparams(("arbitrary",),
                                 io_bytes + 3 * tl * n_seq * 2 * sb * 4 + (12 << 20)),
        name="s5_scan",
    )(*args)


def _glu_kernel(y_ref, w_ref, o_ref):
    y = y_ref[...]
    z = jnp.dot(y.astype(BF16), w_ref[...], preferred_element_type=F32)
    o_ref[...] = (y * jax.nn.sigmoid(z)).astype(o_ref.dtype)


def _glu(y, w, tm):
    m, k = y.shape
    return pl.pallas_call(
        _glu_kernel,
        grid=(m // tm,),
        in_specs=[pl.BlockSpec((tm, k), lambda i: (i, 0)),
                  pl.BlockSpec((k, k), lambda i: (0, 0))],
        out_specs=pl.BlockSpec((tm, k), lambda i: (i, 0)),
        out_shape=jax.ShapeDtypeStruct((m, k), BF16),
        compiler_params=_cparams(("arbitrary",), 2 * k * k * 2 + 6 * tm * k * 4 + (8 << 20)),
        name="glu",
    )(y, w)


def _out_proj_kernel(a1_ref, a2_ref, w1_ref, w2_ref, x_ref, g_ref, o_ref):
    acc = jnp.dot(a1_ref[...], w1_ref[...], preferred_element_type=F32)
    acc = acc + jnp.dot(a2_ref[...], w2_ref[...], preferred_element_type=F32)
    o_ref[...] = x_ref[...] + g_ref[...] * acc


def _out_proj(a1, a2, w, x, gate, tm, tn):
    bsz, t, d = x.shape
    k1, k2 = a1.shape[-1], a2.shape[-1]
    a1 = a1.reshape(bsz, t, k1)
    a2 = a2.reshape(bsz, t, k2)
    if gate.shape[1] == 1:
        g_spec = pl.BlockSpec((None, 1, tn), lambda b, i, j: (b, 0, j))
    else:
        g_spec = pl.BlockSpec((None, tm, tn), lambda b, i, j: (b, i, j))
    return pl.pallas_call(
        _out_proj_kernel,
        grid=(bsz, t // tm, d // tn),
        in_specs=[pl.BlockSpec((None, tm, k1), lambda b, i, j: (b, i, 0)),
                  pl.BlockSpec((None, tm, k2), lambda b, i, j: (b, i, 0)),
                  pl.BlockSpec((k1, tn), lambda b, i, j: (0, j)),
                  pl.BlockSpec((k2, tn), lambda b, i, j: (k1 // k2, j)),
                  pl.BlockSpec((None, tm, tn), lambda b, i, j: (b, i, j)),
                  g_spec],
        out_specs=pl.BlockSpec((None, tm, tn), lambda b, i, j: (b, i, j)),
        out_shape=jax.ShapeDtypeStruct((bsz, t, d), F32),
        compiler_params=_cparams(("arbitrary", "arbitrary", "arbitrary"),
                                 2 * (tm * (k1 + k2) * 2 + (k1 + k2) * tn * 2 + 3 * tm * tn * 4)
                                 + (8 << 20)),
        name="out_proj",
    )(a1, a2, w, w, x, gate)


def _top_rows(s, k):
    n_rows = s.shape[0]
    row = lax.broadcasted_iota(jnp.int32, s.shape, 0).astype(F32)
    k_iota = lax.broadcasted_iota(jnp.int32, (k, s.shape[1]), 0)
    rank = jnp.full(s.shape, NOT_SELECTED, F32)
    val_mat = jnp.zeros((k, s.shape[1]), F32)
    vals, idxs = [], []
    for kk in range(k):
        mx = jnp.max(s, axis=0, keepdims=True)
        idx = jnp.min(jnp.where(s == mx, row, float(n_rows)), axis=0, keepdims=True)
        hit = row == idx
        rank = jnp.where(hit, float(kk), rank)
        s = jnp.where(hit, -jnp.inf, s)
        val_mat = jnp.where(k_iota == kk, mx, val_mat)
        vals.append(mx)
        idxs.append(idx)
    return vals, val_mat, idxs, rank


def _peer_route_kernel(q_ref, keys_ref, p0_ref, th_ref, p1_ref, r1_ref):
    s0 = _dot_nt(keys_ref[0], q_ref[:, :PEER_DHALF])
    s1 = _dot_nt(keys_ref[1], q_ref[:, PEER_DHALF:])
    v0, _, _, rank0 = _top_rows(s0, PEER_TOPK)
    v1, sv1, _, rank1 = _top_rows(s1, PEER_TOPK)
    tm = s0.shape[1]
    n_b = [PEER_TOPK // (a + 1) for a in range(PEER_TOPK)]
    n_cand = sum(n_b)
    pad = (-n_cand) % SUBLANES
    parts = [v0[a] + sv1[:n_b[a]] for a in range(PEER_TOPK)]
    if pad:
        parts.append(jnp.full((pad, tm), -jnp.inf, F32))
    top_s, _, _, cand_rank = _top_rows(jnp.concatenate(parts, axis=0), PEER_TOPK)
    taken = jnp.where(cand_rank < float(PEER_TOPK), 1.0, 0.0)
    z = jnp.zeros((1, tm), F32)
    for kk in range(PEER_TOPK):
        z = z + jnp.exp(top_s[kk] - top_s[0])
    th = jnp.zeros(s0.shape, F32)
    start = 0
    for a in range(PEER_TOPK):
        n_sel = jnp.sum(taken[start:start + n_b[a]], axis=0, keepdims=True)
        th = jnp.where(rank0 == float(a), n_sel, th)
        start += n_b[a]
    p0_ref[...] = jnp.exp(s0 - v0[0])
    th_ref[...] = th
    p1_ref[...] = (jnp.exp(s1 - v1[0]) / z).astype(p1_ref.dtype)
    r1_ref[...] = rank1.astype(r1_ref.dtype)


def _peer_route(q, keys, tm):
    n = q.shape[0]
    dk = 2 * PEER_DHALF
    spec_out = pl.BlockSpec((None, PEER_NKEYS, tm), lambda i, h: (h, 0, i))
    shapes_out = [jax.ShapeDtypeStruct((PEER_HEADS, PEER_NKEYS, n), dt)
                  for dt in (F32, F32, BF16, BF16)]
    return pl.pallas_call(
        _peer_route_kernel,
        grid=(n // tm, PEER_HEADS),
        in_specs=[pl.BlockSpec((tm, dk), lambda i, h: (i, h)),
                  pl.BlockSpec((None, 2, PEER_NKEYS, PEER_DHALF), lambda i, h: (h, 0, 0, 0))],
        out_specs=[spec_out] * 4,
        out_shape=shapes_out,
        compiler_params=_cparams(("arbitrary", "arbitrary"),
                                 4 * tm * dk * 2 + 48 * PEER_NKEYS * tm * 4 + (8 << 20)),
        name="peer_route",
    )(q, keys)


BF16_ROWS = 2 * SUBLANES


def _rows_bf16(row, n_rows):
    tile = jnp.broadcast_to(row, (BF16_ROWS, row.shape[1])).astype(BF16)
    return jnp.concatenate([tile] * (n_rows // BF16_ROWS), axis=0)


def _peer_dense_kernel(h_ref, u_ref, vt_ref, p0_ref, th_ref, p1_ref, r1_ref, o_ref, w_ref, *, te):
    e = pl.program_id(1)

    @pl.when(e == 0)
    def _():
        o_ref[...] = jnp.zeros(o_ref.shape, F32)

    for ib in range(te // PEER_NKEYS):
        i0 = e * (te // PEER_NKEYS) + ib
        th_rows = [th_ref[h, pl.ds(i0, 1), :] for h in range(PEER_HEADS)]
        p0_rows = [p0_ref[h, pl.ds(i0, 1), :] for h in range(PEER_HEADS)]
        for c in range(w_ref.shape[1] // LANES):
            tok = slice(c * LANES, (c + 1) * LANES)
            w = jnp.zeros((PEER_NKEYS, LANES), BF16)
            for h in range(PEER_HEADS):
                th = _rows_bf16(th_rows[h][:, tok], PEER_NKEYS)
                p0 = _rows_bf16(p0_rows[h][:, tok], PEER_NKEYS)
                w = w + jnp.where(r1_ref[h, :, tok] < th, p1_ref[h, :, tok] * p0,
                                  jnp.zeros_like(w))
            w_ref[ib * PEER_NKEYS:(ib + 1) * PEER_NKEYS, tok] = w

    act = jax.nn.gelu(_dot_nt(u_ref[...], h_ref[...]))
    coef = (w_ref[...].astype(F32) * act).astype(BF16)
    o_ref[...] += jnp.dot(vt_ref[...], coef, preferred_element_type=F32)


def _peer_dense(h2, u, v_t, sel, tm, te):
    n, d = h2.shape
    n_exp = u.shape[0]
    once = pl.Buffered(1)
    sel_spec = pl.BlockSpec((PEER_HEADS, PEER_NKEYS, tm), lambda i, e: (0, 0, i),
                            pipeline_mode=once)
    return pl.pallas_call(
        functools.partial(_peer_dense_kernel, te=te),
        grid=(n // tm, n_exp // te),
        in_specs=[pl.BlockSpec((tm, d), lambda i, e: (i, 0), pipeline_mode=once),
                  pl.BlockSpec((te, d), lambda i, e: (e, 0)),
                  pl.BlockSpec((d, te), lambda i, e: (0, e))] + [sel_spec] * 4,
        out_specs=pl.BlockSpec((d, tm), lambda i, e: (0, i), pipeline_mode=once),
        out_shape=jax.ShapeDtypeStruct((d, n), F32),
        scratch_shapes=[pltpu.VMEM((te, tm), BF16)],
        compiler_params=_cparams(
            ("arbitrary", "arbitrary"),
            tm * d * 2 + 2 * 2 * te * d * 2 + 4 * PEER_HEADS * PEER_NKEYS * tm * 4 + d * tm * 4
            + 8 * te * tm * 4 + (6 << 20)),
        name="peer_dense",
    )(h2, u, v_t, *sel)


def _peer_out_kernel(ot_ref, x_ref, g_ref, fg_ref, y_ref):
    x2 = x_ref[...] + g_ref[...] * ot_ref[...].T
    y_ref[...] = _rms(x2) * fg_ref[...]


def _peer_out(o_t, x, gate, final_g, tm):
    bsz, t, d = x.shape
    nt = t // tm
    return pl.pallas_call(
        _peer_out_kernel,
        grid=(bsz, nt),
        in_specs=[pl.BlockSpec((d, tm), lambda b, i: (0, b * nt + i)),
                  pl.BlockSpec((None, tm, d), lambda b, i: (b, i, 0)),
                  _mod_spec(gate, tm, d),
                  pl.BlockSpec((1, d), lambda b, i: (0, 0))],
        out_specs=pl.BlockSpec((None, tm, d), lambda b, i: (b, i, 0)),
        out_shape=jax.ShapeDtypeStruct((bsz, t, d), F32),
        compiler_params=_cparams(("arbitrary", "arbitrary"), 10 * tm * d * 4 + (8 << 20)),
        name="peer_out",
    )(o_t, x, gate, final_g.reshape(1, d))


def _alibi_slopes(n):
    return tuple(float(np.float32(2.0 ** (-8.0 * (h + 1) / n))) for h in range(n))


def _s5_params(lam_re, lam_im, log_dt, b_re, b_im, c_re, c_im, d_skip, seg_len):
    dt = jnp.exp(log_dt.astype(F32))[:, None]
    lr, li = lam_re.astype(F32), lam_im.astype(F32)
    mag = jnp.exp(lr * dt)
    a_re, a_im = mag * jnp.cos(li * dt), mag * jnp.sin(li * dt)
    den = lr * lr + li * li
    nr, ni = a_re - 1.0, a_im
    f_re, f_im = (nr * lr + ni * li) / den, (ni * lr - nr * li) / den
    br, bi = b_re.astype(F32), b_im.astype(F32)
    bb_re = f_re[..., None] * br - f_im[..., None] * bi
    bb_im = f_re[..., None] * bi + f_im[..., None] * br
    mag_s = jnp.exp(lr * dt * seg_len)
    s_re, s_im = mag_s * jnp.cos(li * dt * seg_len), mag_s * jnp.sin(li * dt * seg_len)

    g = lam_re.shape[0]
    nblk = g // GROUPS_PER_BLOCK
    eye = jnp.eye(GROUPS_PER_BLOCK, dtype=F32)

    def in_blocks(w):
        w = w.reshape(nblk, GROUPS_PER_BLOCK, SSM_STATE, SSM_GROUP)
        return jnp.einsum('jgnc,gk->jgckn', w, eye).reshape(nblk, LANES, STATE_BLOCK)

    def out_blocks(w):
        w = w.reshape(nblk, GROUPS_PER_BLOCK, SSM_GROUP, SSM_STATE)
        return jnp.einsum('jgcn,gk->jgnkc', w, eye).reshape(nblk, STATE_BLOCK, LANES)

    bb_blk = jnp.concatenate([in_blocks(bb_re), in_blocks(bb_im)], axis=2).astype(BF16)
    c_blk = jnp.concatenate([out_blocks(c_re.astype(F32)), -out_blocks(c_im.astype(F32))],
                            axis=1).astype(BF16)
    a_blk = jnp.stack([a_re.reshape(nblk, STATE_BLOCK), a_im.reshape(nblk, STATE_BLOCK)], axis=1)
    aseg_blk = jnp.stack([s_re.reshape(nblk, STATE_BLOCK), s_im.reshape(nblk, STATE_BLOCK)], axis=1)
    d_blk = d_skip.astype(F32).reshape(nblk, 1, LANES)
    return bb_blk, a_blk, aseg_blk, c_blk, d_blk


def _pick(n, pref):
    t = min(n, pref)
    while n % t:
        t //= 2
    return t


def kernel(x_prompt, x_sample, cache_k, cache_v, page_table, state_ssm_re, state_ssm_im,
           c_prompt, c_sample, w_ada, b_ada, norm1_g, norm2_g, w_in,
           lam_q1, lam_k1, lam_q2, lam_k2, subln_g,
           ssm_lam_re, ssm_lam_im, ssm_log_dt, ssm_b_re, ssm_b_im, ssm_c_re, ssm_c_im, ssm_d,
           w_glu, w_out, peer_wq, peer_sub_keys, peer_u, peer_v, final_g):
    depth = w_ada.shape[0]
    assert depth == 1, "kernel() fuses the final norm into the layer and supports DEPTH == 1"
    bsz, seq, d_model = x_prompt.shape
    dec_b, dec_t, _ = x_sample.shape
    n_p, n_s = bsz * seq, dec_b * dec_t
    n_pages, page = page_table.shape[1], cache_k.shape[2]
    past_len = n_pages * page
    attn_w = ATTN_HEADS * ATTN_DV
    ssm_w = w_in.shape[2] - 3 * attn_w
    n_groups = ssm_w // SSM_GROUP
    slopes = _alibi_slopes(ATTN_HEADS)
    seg_len = seq // PROMPT_SEGMENTS
    n_seq_p = bsz * PROMPT_SEGMENTS

    xp = x_prompt
    xs = x_sample.reshape(1, n_s, d_model)
    outs = [[] for _ in range(8)]

    for l in range(depth):
        lam_init = 0.8 - 0.6 * math.exp(-0.3 * l)
        out_scale = 1.0 - lam_init
        lam = (jnp.exp(jnp.sum(lam_q1[l].astype(F32) * lam_k1[l].astype(F32)))
               - jnp.exp(jnp.sum(lam_q2[l].astype(F32) * lam_k2[l].astype(F32))) + lam_init)
        scalars = jnp.concatenate([lam.reshape(1), jnp.asarray(slopes, F32)])

        c_all = jnp.concatenate([c_prompt, c_sample], axis=0)
        pad = (-c_all.shape[0]) % SUBLANES
        c_all = jnp.pad(c_all, ((0, pad), (0, 0)))
        mod = _adaln(c_all, w_ada[l], b_ada[l])
        mod_p = [m[:, None, :] for m in jnp.split(mod[:bsz], N_MOD, axis=-1)]
        mod_s = [jnp.repeat(m, dec_t, axis=0)[None] for m in
                 jnp.split(mod[bsz:bsz + dec_b], N_MOD, axis=-1)]

        w_in_b = w_in[l].astype(BF16)
        w_glu_b = w_glu[l].astype(BF16)
        w_out_b = w_out[l].astype(BF16)
        wq_b = peer_wq[l].astype(BF16)
        keys_b = peer_sub_keys[l].astype(BF16)
        u_b = peer_u[l].astype(BF16)
        v_t = peer_v[l].T.astype(BF16)
        bb_blk, a_blk, aseg_blk, c_blk, d_blk = _s5_params(
            ssm_lam_re[l], ssm_lam_im[l], ssm_log_dt[l], ssm_b_re[l], ssm_b_im[l],
            ssm_c_re[l], ssm_c_im[l], ssm_d[l], seg_len)

        def front(x, mods, tm):
            sh1, sc1 = mods[0], mods[1]
            h = _norm_mod(x, norm1_g[l], sh1, sc1, tm).reshape(-1, d_model)
            tmm = _pick(h.shape[0], 1024)
            (q_b, q_f) = _matmul_cols(h, w_in_b, 0, attn_w, (BF16, F32), tmm, 512)
            (k_f, k_b) = _matmul_cols(h, w_in_b, attn_w, attn_w, (F32, BF16), tmm, 512)
            (v_f, v_b) = _matmul_cols(h, w_in_b, 2 * attn_w, attn_w, (F32, BF16), tmm, 512)
            (u_f,) = _matmul_cols(h, w_in_b, 3 * attn_w, ssm_w, (F32,), tmm, 512)
            return q_b, q_f, k_f, k_b, v_f, v_b, u_f

        def back(x, mods, o_attn, o_ssm, tm):
            g1, sh2, sc2, g2 = mods[2], mods[3], mods[4], mods[5]
            n = x.shape[0] * x.shape[1]
            x1 = _out_proj(o_attn, o_ssm, w_out_b, x, g1, _pick(x.shape[1], 1024), 512)
            h2 = _norm_mod(x1, norm2_g[l], sh2, sc2, tm).reshape(n, d_model)
            tmp = _pick(n, 512)
            (q_peer,) = _matmul_cols(h2, wq_b, 0, wq_b.shape[1], (BF16,), _pick(n, 1024), 512)
            sel = _peer_route(q_peer, keys_b, tmp)
            o_t = _peer_dense(h2, u_b, v_t, sel, tmp, 512)
            return _peer_out(o_t, x1, g2, final_g, _pick(x.shape[1], 256))

        q_b, _, k_f, k_b, v_f, v_b, u_f = front(xp, mod_p, 256)
        o_attn_p = _prompt_attention(scalars, q_b, k_b, v_b, subln_g[l], bsz, seq, out_scale,
                                     256, 8)
        u_tm = u_f.reshape(bsz, PROMPT_SEGMENTS, seg_len, ssm_w).transpose(2, 0, 1, 3)
        u_tm = u_tm.reshape(seg_len, n_seq_p, ssm_w)
        zeros_p = jnp.zeros((2, n_seq_p, n_groups * SSM_STATE), F32)
        (seg_end,) = _s5_scan(u_tm, bb_blk, a_blk, zeros_p, None, None, None,
                              tl=16, chain=False, emit_y=False)
        y_tm, h_end = _s5_scan(u_tm, bb_blk, a_blk, seg_end, aseg_blk, c_blk, d_blk,
                               tl=16, chain=True, emit_y=True)
        y_p = y_tm.reshape(seg_len, bsz, PROMPT_SEGMENTS, ssm_w).transpose(1, 2, 0, 3)
        o_ssm_p = _glu(y_p.reshape(n_p, ssm_w), w_glu_b, 256)
        y_prompt_l = back(xp, mod_p, o_attn_p, o_ssm_p, 256)
        h_end = h_end.reshape(2, bsz, PROMPT_SEGMENTS, n_groups, SSM_STATE)[:, :, -1]
        outs[0].append(k_f.reshape(bsz, seq, ATTN_HEADS, ATTN_DV))
        outs[1].append(v_f.reshape(bsz, seq, ATTN_HEADS, ATTN_DV))
        outs[4].append(h_end[0])
        outs[5].append(h_end[1])

        _, q_f, k_f, _, v_f, _, u_f = front(xs, mod_s, n_s)
        qh = q_f.reshape(dec_b, dec_t, ATTN_HEADS, ATTN_DV).transpose(0, 2, 1, 3)
        lane = jnp.arange(ATTN_DV) < ATTN_DH
        qh = jnp.stack([jnp.where(lane, qh, 0.0), jnp.where(lane, 0.0, qh)], axis=2)
        qh = jnp.pad(qh, ((0, 0), (0, 0), (0, 0), (0, SUBLANES - dec_t), (0, 0)))
        q_rows = (qh * (ATTN_DH ** -0.5)).astype(BF16).reshape(
            dec_b, ATTN_HEADS * ROWS_PER_HEAD, ATTN_DV)
        kv_view = (-1, page, HEAD_TILES, SUBLANES, ATTN_DV)
        new_pad = ((0, 0), (0, page - dec_t), (0, 0), (0, 0), (0, 0))
        k_new = jnp.pad(k_f.reshape((dec_b, dec_t) + kv_view[2:]), new_pad)
        v_new = jnp.pad(v_f.reshape((dec_b, dec_t) + kv_view[2:]), new_pad)
        o_rows = _decode_attention(
            page_table, scalars, q_rows, cache_k.reshape(kv_view), cache_v.reshape(kv_view),
            k_new, v_new, subln_g[l], page, past_len, out_scale, slopes, 8)
        o_attn_s = o_rows.reshape(dec_b, ATTN_HEADS, SUBLANES, ATTN_DV)[:, :, :dec_t]
        o_attn_s = o_attn_s.transpose(0, 2, 1, 3).reshape(n_s, attn_w).astype(BF16)
        u_tm = u_f.reshape(dec_b, dec_t, ssm_w).transpose(1, 0, 2)
        h0 = jnp.stack([state_ssm_re[l].reshape(dec_b, -1), state_ssm_im[l].reshape(dec_b, -1)])
        y_tm, h_end = _s5_scan(u_tm, bb_blk, a_blk, h0.astype(F32), None, c_blk, d_blk,
                               tl=dec_t, chain=False, emit_y=True)
        o_ssm_s = _glu(y_tm.transpose(1, 0, 2).reshape(n_s, ssm_w), w_glu_b, n_s)
        y_sample_l = back(xs, mod_s, o_attn_s, o_ssm_s, n_s)
        outs[2].append(k_f.reshape(dec_b, dec_t, ATTN_HEADS, ATTN_DV))
        outs[3].append(v_f.reshape(dec_b, dec_t, ATTN_HEADS, ATTN_DV))
        outs[6].append(h_end[0].reshape(dec_b, n_groups, SSM_STATE))
        outs[7].append(h_end[1].reshape(dec_b, n_groups, SSM_STATE))

    y_prompt = y_prompt_l
    y_sample = y_sample_l.reshape(dec_b, dec_t, d_model)
    return (y_prompt, y_sample) + tuple(jnp.stack(o) for o in outs)
```
